```python
import jax, jax.numpy as jnp
from jax import lax
import numpy as np

D_MODEL = 1024
BATCH = 2
SEQ = 16384
DEPTH = 1
DEC_BATCH = 16
DEC_SEQ = 16
PAST_LEN = 2048

CHUNK = 64
N_PREV_CHUNKS = 8
ATT_PAST = N_PREV_CHUNKS * CHUNK
BAND = ATT_PAST + CHUNK
D_MIX = D_MODEL
D_ATT = D_MIX // 2
ATT_HEADS = 8
ATT_HEAD_DIM = D_ATT // ATT_HEADS
D_CONV = D_MIX - D_ATT
CONV_WIDTH = 31
REL_CLIP = 128
N_MEM = 256
MEM_HEADS = 4
MEM_HEAD_DIM = D_MODEL // MEM_HEADS
D_FF = -(-8 * D_MODEL // (3 * 256)) * 256
D_IN = 3 * D_ATT + 2 * D_CONV
EPS = 1e-6
NEG_INF = -1e30

kernel_name = "hybrid_streaming_encoder_step"


def rmsnorm(x, g):
    xf = x.astype(jnp.float32)
    y = xf * lax.rsqrt(jnp.mean(xf * xf, axis=-1, keepdims=True) + EPS)
    return (y * g.astype(jnp.float32)).astype(x.dtype)


def layernorm(x, g, b):
    xf = x.astype(jnp.float32)
    mu = jnp.mean(xf, axis=-1, keepdims=True)
    xc = xf - mu
    var = jnp.mean(xc * xc, axis=-1, keepdims=True)
    y = xc * lax.rsqrt(var + EPS) * g.astype(jnp.float32) + b.astype(jnp.float32)
    return y.astype(x.dtype)


def band_attention(q, k_ext, v_ext, key_valid, rel_bias):
    n_chunks = q.shape[1] // CHUNK
    qi = jnp.arange(CHUNK)[:, None]
    kj = jnp.arange(BAND)[None, :]
    dist = jnp.clip(qi + ATT_PAST - kj, -REL_CLIP, REL_CLIP) + REL_CLIP
    bias = rel_bias[:, dist].astype(jnp.float32)
    scale = ATT_HEAD_DIM ** -0.5

    def one_chunk(c):
        start = c * CHUNK
        qc = lax.dynamic_slice_in_dim(q, start, CHUNK, axis=1)
        kc = lax.dynamic_slice_in_dim(k_ext, start, BAND, axis=1)
        vc = lax.dynamic_slice_in_dim(v_ext, start, BAND, axis=1)
        valid = lax.dynamic_slice_in_dim(key_valid, start, BAND)
        s = jnp.einsum("bqhd,bkhd->bhqk", qc, kc, preferred_element_type=jnp.float32) * scale + bias
        s = jnp.where(valid[None, None, None, :], s, NEG_INF)
        p = jax.nn.softmax(s, axis=-1)
        return jnp.einsum("bhqk,bkhd->bqhd", p.astype(vc.dtype), vc)

    out = lax.map(one_chunk, jnp.arange(n_chunks))
    return jnp.moveaxis(out, 0, 1).reshape(q.shape)


def parallel_mixer(h, past_k, past_v, n_past_valid, keep, conv_buf,
                   w_in, rel_bias, conv_w, conv_b, cln_g, cln_b, w_out):
    B, T, _ = h.shape
    proj = h @ w_in
    q, k, v, u_val, u_gate = jnp.split(
        proj, [D_ATT, 2 * D_ATT, 3 * D_ATT, 3 * D_ATT + D_CONV], axis=-1)
    q = q.reshape(B, T, ATT_HEADS, ATT_HEAD_DIM)
    k = k.reshape(B, T, ATT_HEADS, ATT_HEAD_DIM)
    v = v.reshape(B, T, ATT_HEADS, ATT_HEAD_DIM)

    t_pad = (-T) % CHUNK
    pad = lambda a: jnp.pad(a, ((0, 0), (0, t_pad), (0, 0), (0, 0)))
    k_ext = jnp.concatenate([past_k, pad(k)], axis=1)
    v_ext = jnp.concatenate([past_v, pad(v)], axis=1)
    key_valid = jnp.concatenate([jnp.arange(ATT_PAST) >= ATT_PAST - n_past_valid,
                                 jnp.arange(T + t_pad) < T])
    att = band_attention(pad(q), k_ext, v_ext, key_valid, rel_bias)[:, :T]
    att = att.reshape(B, T, D_ATT)
    end = ATT_PAST + T
    new_k = k_ext[:, end - keep:end]
    new_v = v_ext[:, end - keep:end]

    u = u_val * jax.nn.sigmoid(u_gate)
    u_ext = jnp.concatenate([conv_buf, u], axis=1)
    c = lax.conv_general_dilated(
        u_ext, conv_w[:, None, :], window_strides=(1,), padding="VALID",
        dimension_numbers=("NWC", "WIO", "NWC"), feature_group_count=D_CONV) + conv_b
    c = jax.nn.silu(layernorm(c, cln_g, cln_b))
    new_conv = u_ext[:, -(CONV_WIDTH - 1):]

    y = jnp.concatenate([att, c], axis=-1) @ w_out
    return y, new_k, new_v, new_conv


def memory_kv(mem, g_mem, w_mk, w_mv):
    B = mem.shape[0]
    m = rmsnorm(mem, g_mem)
    mk = (m @ w_mk).reshape(B, N_MEM, MEM_HEADS, MEM_HEAD_DIM)
    mv = (m @ w_mv).reshape(B, N_MEM, MEM_HEADS, MEM_HEAD_DIM)
    return mk, mv


def memory_attention(h, mk, mv, w_mq, w_mo):
    B, T, _ = h.shape
    q = (h @ w_mq).reshape(B, T, MEM_HEADS, MEM_HEAD_DIM)
    s = jnp.einsum("bthd,bmhd->bhtm", q, mk, preferred_element_type=jnp.float32) * (MEM_HEAD_DIM ** -0.5)
    p = jax.nn.softmax(s, axis=-1)
    o = jnp.einsum("bhtm,bmhd->bthd", p.astype(mv.dtype), mv).reshape(B, T, D_MODEL)
    return o @ w_mo


def encoder_layer(x, past_k, past_v, n_past_valid, keep, conv_buf, mk, mv,
                  g_mix_pre, g_mix_post, w_in, rel_bias, conv_w, conv_b, cln_g, cln_b, w_out,
                  g_mem_pre, g_mem_post, w_mq, w_mo,
                  g_ffn_pre, g_ffn_post, w_gate, w_up, w_down):
    h = rmsnorm(x, g_mix_pre)
    y, new_k, new_v, new_conv = parallel_mixer(h, past_k, past_v, n_past_valid, keep, conv_buf,
                                               w_in, rel_bias, conv_w, conv_b, cln_g, cln_b, w_out)
    x = x + rmsnorm(y, g_mix_post)
    h = rmsnorm(x, g_mem_pre)
    x = x + rmsnorm(memory_attention(h, mk, mv, w_mq, w_mo), g_mem_post)
    h = rmsnorm(x, g_ffn_pre)
    f = (jax.nn.silu(h @ w_gate) * (h @ w_up)) @ w_down
    x = x + rmsnorm(f, g_ffn_post)
    return x, new_k, new_v, new_conv


def setup_inputs(seed: int = 0) -> dict:
    key = jax.random.key(seed)
    ks = iter(jax.random.split(key, 40))
    f32 = jnp.float32
    nrm = lambda shape, s: jax.random.normal(next(ks), shape, f32) * s
    gain = lambda shape: 1.0 + nrm(shape, 0.02)
    R = min(ATT_PAST, PAST_LEN)
    return {
        "x_prompt": nrm((BATCH, SEQ, D_MODEL), 1.0),
        "x_sample": nrm((DEC_BATCH, DEC_SEQ, D_MODEL), 1.0),
        "cache_att_k": nrm((DEPTH, DEC_BATCH, R, ATT_HEADS, ATT_HEAD_DIM), 1.0),
        "cache_att_v": nrm((DEPTH, DEC_BATCH, R, ATT_HEADS, ATT_HEAD_DIM), 1.0),
        "cache_conv": nrm((DEPTH, DEC_BATCH, CONV_WIDTH - 1, D_CONV), 0.5),
        "cache_mem_k": nrm((DEPTH, DEC_BATCH, N_MEM, MEM_HEADS, MEM_HEAD_DIM), 1.0),
        "cache_mem_v": nrm((DEPTH, DEC_BATCH, N_MEM, MEM_HEADS, MEM_HEAD_DIM), 1.0),
        "mem_prompt": nrm((BATCH, N_MEM, D_MODEL), 1.0),
        "g_mix_pre": gain((DEPTH, D_MODEL)),
        "g_mix_post": gain((DEPTH, D_MODEL)),
        "w_in": nrm((DEPTH, D_MODEL, D_IN), D_MODEL ** -0.5),
        "rel_bias": nrm((DEPTH, ATT_HEADS, 2 * REL_CLIP + 1), 0.1),
        "conv_w": nrm((DEPTH, CONV_WIDTH, D_CONV), CONV_WIDTH ** -0.5),
        "conv_b": nrm((DEPTH, D_CONV), 0.01),
        "cln_g": gain((DEPTH, D_CONV)),
        "cln_b": nrm((DEPTH, D_CONV), 0.01),
        "w_out": nrm((DEPTH, D_MIX, D_MODEL), D_MIX ** -0.5),
        "g_mem_pre": gain((DEPTH, D_MODEL)),
        "g_mem_post": gain((DEPTH, D_MODEL)),
        "g_mem_kv": gain((DEPTH, D_MODEL)),
        "w_mq": nrm((DEPTH, D_MODEL, D_MODEL), D_MODEL ** -0.5),
        "w_mk": nrm((DEPTH, D_MODEL, D_MODEL), D_MODEL ** -0.5),
        "w_mv": nrm((DEPTH, D_MODEL, D_MODEL), D_MODEL ** -0.5),
        "w_mo": nrm((DEPTH, D_MODEL, D_MODEL), D_MODEL ** -0.5),
        "g_ffn_pre": gain((DEPTH, D_MODEL)),
        "g_ffn_post": gain((DEPTH, D_MODEL)),
        "w_gate": nrm((DEPTH, D_MODEL, D_FF), D_MODEL ** -0.5),
        "w_up": nrm((DEPTH, D_MODEL, D_FF), D_MODEL ** -0.5),
        "w_down": nrm((DEPTH, D_FF, D_MODEL), D_FF ** -0.5),
    }


def reference(x_prompt, x_sample, cache_att_k, cache_att_v, cache_conv, cache_mem_k, cache_mem_v,
              mem_prompt, g_mix_pre, g_mix_post, w_in, rel_bias, conv_w, conv_b, cln_g, cln_b, w_out,
              g_mem_pre, g_mem_post, g_mem_kv, w_mq, w_mk, w_mv, w_mo,
              g_ffn_pre, g_ffn_post, w_gate, w_up, w_down):
    B, T_p, _ = x_prompt.shape
    Bs, T_s, _ = x_sample.shape
    R = cache_att_k.shape[2]
    keep_prompt = min(ATT_PAST, T_p)
    xp, xs = x_prompt, x_sample
    akp, avp, cvp, mkp, mvp, aks, avs, cvs = [], [], [], [], [], [], [], []
    for l in range(DEPTH):
        w = (g_mix_pre[l], g_mix_post[l], w_in[l], rel_bias[l], conv_w[l], conv_b[l], cln_g[l], cln_b[l],
             w_out[l], g_mem_pre[l], g_mem_post[l], w_mq[l], w_mo[l],
             g_ffn_pre[l], g_ffn_post[l], w_gate[l], w_up[l], w_down[l])
        zeros_kv = jnp.zeros((B, ATT_PAST, ATT_HEADS, ATT_HEAD_DIM), xp.dtype)
        zeros_conv = jnp.zeros((B, CONV_WIDTH - 1, D_CONV), xp.dtype)
        mk_p, mv_p = memory_kv(mem_prompt, g_mem_kv[l], w_mk[l], w_mv[l])
        xp, nk_p, nv_p, nc_p = encoder_layer(xp, zeros_kv, zeros_kv, 0, keep_prompt, zeros_conv,
                                             mk_p, mv_p, *w)
        past_pad = ((0, 0), (ATT_PAST - R, 0), (0, 0), (0, 0))
        pk = jnp.pad(cache_att_k[l], past_pad)
        pv = jnp.pad(cache_att_v[l], past_pad)
        xs, nk_s, nv_s, nc_s = encoder_layer(xs, pk, pv, R, R, cache_conv[l],
                                             cache_mem_k[l], cache_mem_v[l], *w)
        akp.append(nk_p); avp.append(nv_p); cvp.append(nc_p); mkp.append(mk_p); mvp.append(mv_p)
        aks.append(nk_s); avs.append(nv_s); cvs.append(nc_s)
    new_att_k_prompt = jnp.stack(akp, 0)
    new_att_v_prompt = jnp.stack(avp, 0)
    new_conv_prompt = jnp.stack(cvp, 0)
    new_mem_k_prompt = jnp.stack(mkp, 0)
    new_mem_v_prompt = jnp.stack(mvp, 0)
    new_att_k_sample = jnp.stack(aks, 0)
    new_att_v_sample = jnp.stack(avs, 0)
    new_conv_sample = jnp.stack(cvs, 0)
    return (xp, xs, new_att_k_prompt, new_att_v_prompt, new_conv_prompt, new_mem_k_prompt,
            new_mem_v_prompt, new_att_k_sample, new_att_v_sample, new_conv_sample)
```

```python
import functools

import jax
import jax.numpy as jnp
from jax import lax
from jax.experimental import pallas as pl
from jax.experimental.pallas import tpu as pltpu

D_MODEL = 1024
CHUNK = 64
ATT_PAST = 512
BAND = ATT_PAST + CHUNK
D_ATT = 512
ATT_HEADS = 8
ATT_HEAD_DIM = 64
D_CONV = 512
CONV_WIDTH = 31
REL_CLIP = 128
N_MEM = 256
MEM_HEADS = 4
MEM_HEAD_DIM = 256
D_FF = 2816
EPS = 1e-6
NEG_INF = -1e30

LANES = 128
MXU_COLS = 256
GROUP = 4 * CHUNK
N_KEY_TILES = (ATT_PAST + GROUP) // MXU_COLS
KEY_WINDOW = BAND + CHUNK
HALO = 32
FF_CHUNK = MXU_COLS
VMEM_LIMIT = 56 * 1024 * 1024

F32 = jnp.float32
BF16 = jnp.bfloat16


def _params(n_grid):
    return pltpu.CompilerParams(dimension_semantics=("arbitrary",) * n_grid,
                                vmem_limit_bytes=VMEM_LIMIT)


def _const_spec(shape):
    zeros = (0,) * len(shape)
    return pl.BlockSpec(shape, lambda *_: zeros, pipeline_mode=pl.Buffered(1))


def _dot(a, b):
    return jnp.dot(a, b, preferred_element_type=F32)


def _dot_nt(a, b):
    return lax.dot_general(a, b, (((1,), (1,)), ((), ())), preferred_element_type=F32)


def _rms(x, g):
    y = x * lax.rsqrt(jnp.mean(x * x, axis=-1, keepdims=True) + EPS)
    return y * g


def _mem_kv_kernel(m_ref, g_ref, wk_ref, wv_ref, k_ref, v_ref, kb_ref, vb_ref):
    m = _rms(m_ref[...], g_ref[...]).astype(BF16)
    k = _dot(m, wk_ref[...])
    v = _dot(m, wv_ref[...])
    k_ref[...] = k
    v_ref[...] = v
    kb_ref[...] = k.astype(BF16)
    vb_ref[...] = v.astype(BF16)


def _mem_kv(mem, g, wk, wv):
    b, n, d = mem.shape
    row = pl.BlockSpec((None, n, d), lambda i: (i, 0, 0))
    return pl.pallas_call(
        _mem_kv_kernel,
        grid=(b,),
        in_specs=[row, _const_spec((1, d)), _const_spec((d, d)), _const_spec((d, d))],
        out_specs=[row, row, row, row],
        out_shape=[jax.ShapeDtypeStruct((b, n, d), F32)] * 2
        + [jax.ShapeDtypeStruct((b, n, d), BF16)] * 2,
        compiler_params=_params(1),
        name="mem_kv",
    )(mem, g, wk, wv)


def _mixer_proj(x_ref, g_ref, w_ref):
    h = _rms(x_ref[...], g_ref[...]).astype(BF16)
    q = _dot(h, w_ref[:, 0:D_ATT]) * (ATT_HEAD_DIM ** -0.5)
    k = _dot(h, w_ref[:, D_ATT:2 * D_ATT])
    v = _dot(h, w_ref[:, 2 * D_ATT:3 * D_ATT])
    u_val = _dot(h, w_ref[:, 3 * D_ATT:3 * D_ATT + D_CONV])
    u_gate = _dot(h, w_ref[:, 3 * D_ATT + D_CONV:3 * D_ATT + 2 * D_CONV])
    return q, k, v, u_val * jax.nn.sigmoid(u_gate)


def _mixer_in_prompt_kernel(x_ref, g_ref, w_ref, q_ref, kt_ref, v_ref, u_ref, kl_ref, vl_ref,
                            *, n_t):
    q, k, v, u = _mixer_proj(x_ref, g_ref, w_ref)
    q_ref[...] = q.astype(BF16)
    kt_ref[...] = k.T.astype(BF16)
    v_ref[...] = v.astype(BF16)
    u_ref[...] = u

    @pl.when(pl.program_id(1) == n_t - 1)
    def _():
        kl_ref[...] = k
        vl_ref[...] = v


def _mixer_in_prompt(x, g, w_in):
    b, t, d = x.shape
    tt = ATT_PAST
    n_t = t // tt
    d_in = w_in.shape[1]
    rows = lambda c: pl.BlockSpec((None, tt, c), lambda i, j: (i, j, 0))
    last = pl.BlockSpec((None, tt, D_ATT), lambda i, j: (i, 0, 0))
    return pl.pallas_call(
        functools.partial(_mixer_in_prompt_kernel, n_t=n_t),
        grid=(b, n_t),
        in_specs=[rows(d), _const_spec((1, d)), _const_spec((d, d_in))],
        out_specs=[rows(D_ATT),
                   pl.BlockSpec((None, D_ATT, tt), lambda i, j: (i, 0, j)),
                   rows(D_ATT), rows(D_CONV), last, last],
        out_shape=[jax.ShapeDtypeStruct((b, t, D_ATT), BF16),
                   jax.ShapeDtypeStruct((b, D_ATT, t), BF16),
                   jax.ShapeDtypeStruct((b, t, D_ATT), BF16),
                   jax.ShapeDtypeStruct((b, t, D_CONV), F32),
                   jax.ShapeDtypeStruct((b, tt, D_ATT), F32),
                   jax.ShapeDtypeStruct((b, tt, D_ATT), F32)],
        compiler_params=_params(2),
        name="mixer_in_prompt",
    )(x, g, w_in)


def _mixer_in_sample_kernel(x_ref, g_ref, w_ref, q_ref, k_ref, v_ref, u_ref):
    q, k, v, u = _mixer_proj(x_ref, g_ref, w_ref)
    q_ref[...] = q.astype(BF16)
    k_ref[...] = k
    v_ref[...] = v
    u_ref[...] = u


def _mixer_in_sample(x, g, w_in):
    n, d = x.shape
    d_in = w_in.shape[1]
    full = lambda c: pl.BlockSpec((n, c), lambda i: (0, 0))
    return pl.pallas_call(
        _mixer_in_sample_kernel,
        grid=(1,),
        in_specs=[full(d), _const_spec((1, d)), _const_spec((d, d_in))],
        out_specs=[full(D_ATT), full(D_ATT), full(D_ATT), full(D_CONV)],
        out_shape=[jax.ShapeDtypeStruct((n, D_ATT), BF16),
                   jax.ShapeDtypeStruct((n, D_ATT), F32),
                   jax.ShapeDtypeStruct((n, D_ATT), F32),
                   jax.ShapeDtypeStruct((n, D_CONV), F32)],
        compiler_params=_params(1),
        name="mixer_in_sample",
    )(x, g, w_in)


def _stack_head_pair(q2):
    lane = lax.broadcasted_iota(jnp.int32, q2.shape, 1)
    zero = jnp.zeros_like(q2)
    return jnp.concatenate([jnp.where(lane < ATT_HEAD_DIM, q2, zero),
                            jnp.where(lane >= ATT_HEAD_DIM, q2, zero)], axis=0)


def _merge_head_pair(o):
    m = o.shape[0] // 2
    lane = lax.broadcasted_iota(jnp.int32, (m, LANES), 1)
    return jnp.where(lane < ATT_HEAD_DIM, o[:m], o[m:])


def _band_attn_kernel(q_ref, kt0_ref, kt1_ref, kt2_ref, v0_ref, v1_ref, v2_ref, bias_ref,
                      o_ref, s_buf, p_buf):
    g = pl.program_id(1)
    kt_refs = (kt0_ref, kt1_ref, kt2_ref)
    v_refs = (v0_ref, v1_ref, v2_ref)
    n_chunks = GROUP // CHUNK
    for p in range(ATT_HEADS // 2):
        sl = slice(p * LANES, (p + 1) * LANES)
        qm = _stack_head_pair(q_ref[:, sl])
        for j in range(N_KEY_TILES):
            s = _dot(qm, kt_refs[j][sl, :])
            if j < N_KEY_TILES - 1:
                s = s + jnp.where(g - (N_KEY_TILES - 1) + j >= 0, 0.0, NEG_INF)
            s_buf[:, j * MXU_COLS:(j + 1) * MXU_COLS] = s
        inv_l = []
        for hh in range(2):
            for i in range(n_chunks):
                w0 = 0 if i < n_chunks // 2 else LANES
                r0 = hh * GROUP + i * CHUNK
                sb = s_buf[r0:r0 + CHUNK, w0:w0 + KEY_WINDOW] + bias_ref[2 * p + hh, i]
                m = jnp.max(sb, axis=-1, keepdims=True)
                e = jnp.exp(sb - m)
                inv_l.append(1.0 / jnp.sum(e, axis=-1, keepdims=True))
                p_buf[r0:r0 + CHUNK, w0:w0 + KEY_WINDOW] = e.astype(BF16)
                z0 = KEY_WINDOW if w0 == 0 else 0
                p_buf[r0:r0 + CHUNK, z0:z0 + LANES] = jnp.zeros((CHUNK, LANES), BF16)
        o = _dot(p_buf[:, 0:MXU_COLS], v_refs[0][:, sl])
        for j in range(1, N_KEY_TILES):
            o = o + _dot(p_buf[:, j * MXU_COLS:(j + 1) * MXU_COLS], v_refs[j][:, sl])
        o = o * jnp.concatenate(inv_l, axis=0)
        o_ref[:, sl] = _merge_head_pair(o).astype(BF16)


def _band_bias_prompt(rel_bias):
    n_chunks = GROUP // CHUNK
    i = jnp.arange(n_chunks)[:, None, None]
    qi = jnp.arange(CHUNK)[None, :, None]
    kw = jnp.arange(KEY_WINDOW)[None, None, :]
    w0 = jnp.where(i < n_chunks // 2, 0, LANES)
    kj = w0 + kw - CHUNK * i
    dist = jnp.clip(qi + ATT_PAST - kj, -REL_CLIP, REL_CLIP) + REL_CLIP
    bias = rel_bias[:, dist].astype(F32)
    return jnp.where((kj >= 0) & (kj < BAND), bias, NEG_INF)


def _band_attn_prompt(q, kt, v, bias):
    b, t, _ = q.shape
    n_g = t // GROUP
    back = N_KEY_TILES - 1
    kt_spec = lambda j: pl.BlockSpec(
        (None, D_ATT, MXU_COLS), lambda i, g: (i, 0, jnp.maximum(g - back + j, 0)))
    v_spec = lambda j: pl.BlockSpec(
        (None, MXU_COLS, D_ATT), lambda i, g: (i, jnp.maximum(g - back + j, 0), 0))
    rows = pl.BlockSpec((None, GROUP, D_ATT), lambda i, g: (i, g, 0))
    return pl.pallas_call(
        _band_attn_kernel,
        grid=(b, n_g),
        in_specs=[rows] + [kt_spec(j) for j in range(N_KEY_TILES)]
        + [v_spec(j) for j in range(N_KEY_TILES)] + [_const_spec(bias.shape)],
        out_specs=rows,
        out_shape=jax.ShapeDtypeStruct((b, t, D_ATT), BF16),
        scratch_shapes=[pltpu.VMEM((2 * GROUP, N_KEY_TILES * MXU_COLS), F32),
                        pltpu.VMEM((2 * GROUP, N_KEY_TILES * MXU_COLS), BF16)],
        compiler_params=_params(2),
        name="band_attn_prompt",
    )(q, kt, kt, kt, v, v, v, bias)


def _band_attn_sample_kernel(q_ref, kn_ref, vn_ref, ck_ref, cv_ref, bias_ref,
                             o_ref, nk_ref, nv_ref, k_buf, v_buf, *, t_s):
    past = ck_ref.shape[0]
    n_keys = k_buf.shape[0]
    nk_ref[0:past - t_s, :] = ck_ref[t_s:past, :]
    nk_ref[past - t_s:past, :] = kn_ref[...]
    nv_ref[0:past - t_s, :] = cv_ref[t_s:past, :]
    nv_ref[past - t_s:past, :] = vn_ref[...]

    k_buf[0:past, :] = ck_ref[...].astype(BF16)
    k_buf[past:past + t_s, :] = kn_ref[...].astype(BF16)
    k_buf[past + t_s:n_keys, :] = jnp.zeros((n_keys - past - t_s, D_ATT), BF16)
    v_buf[0:past, :] = cv_ref[...].astype(BF16)
    v_buf[past:past + t_s, :] = vn_ref[...].astype(BF16)
    v_buf[past + t_s:n_keys, :] = jnp.zeros((n_keys - past - t_s, D_ATT), BF16)

    for p in range(ATT_HEADS // 2):
        sl = slice(p * LANES, (p + 1) * LANES)
        qm = _stack_head_pair(q_ref[:, sl])
        bias = jnp.concatenate([bias_ref[2 * p], bias_ref[2 * p + 1]], axis=0)
        s = _dot_nt(qm, k_buf[:, sl]) + bias
        m = jnp.max(s, axis=-1, keepdims=True)
        e = jnp.exp(s - m)
        inv_l = 1.0 / jnp.sum(e, axis=-1, keepdims=True)
        o = _dot(e.astype(BF16), v_buf[:, sl]) * inv_l
        o_ref[:, sl] = _merge_head_pair(o).astype(BF16)


def _band_bias_sample(rel_bias, t_s, n_keys):
    qi = jnp.arange(t_s)[:, None]
    kj = jnp.arange(n_keys)[None, :]
    dist = jnp.clip(qi + ATT_PAST - kj, -REL_CLIP, REL_CLIP) + REL_CLIP
    bias = rel_bias[:, dist].astype(F32)
    return jnp.where(kj < ATT_PAST + t_s, bias, NEG_INF)


def _band_attn_sample(q, k_new, v_new, cache_k, cache_v, rel_bias):
    b, t_s, _ = q.shape
    past = cache_k.shape[1]
    n_keys = past + LANES
    bias = _band_bias_sample(rel_bias, t_s, n_keys)
    new = pl.BlockSpec((None, t_s, D_ATT), lambda i: (i, 0, 0))
    buf = pl.BlockSpec((None, past, D_ATT), lambda i: (i, 0, 0))
    return pl.pallas_call(
        functools.partial(_band_attn_sample_kernel, t_s=t_s),
        grid=(b,),
        in_specs=[new, new, new, buf, buf, _const_spec(bias.shape)],
        out_specs=[new, buf, buf],
        out_shape=[jax.ShapeDtypeStruct((b, t_s, D_ATT), BF16),
                   jax.ShapeDtypeStruct((b, past, D_ATT), F32),
                   jax.ShapeDtypeStruct((b, past, D_ATT), F32)],
        scratch_shapes=[pltpu.VMEM((n_keys, D_ATT), BF16), pltpu.VMEM((n_keys, D_ATT), BF16)],
        compiler_params=_params(1),
        name="band_attn_sample",
    )(q, k_new, v_new, cache_k, cache_v, bias)


def _mix_out_kernel(att_ref, u_ref, halo_ref, x_ref, cw_ref, cb_ref, lg_ref, lb_ref, wo_ref,
                    gp_ref, o_ref, u_buf, c_buf, *, tt, zero_first):
    halo = halo_ref[...]
    if zero_first:
        halo = jnp.where(pl.program_id(1) > 0, halo, 0.0)
    u_buf[0:HALO, :] = halo
    u_buf[HALO:HALO + tt, :] = u_ref[...]

    lead = HALO - (CONV_WIDTH - 1)
    rs = min(tt, CHUNK)
    for r0 in range(0, tt, rs):
        acc = jnp.zeros((rs, D_CONV), F32) + cb_ref[...]
        for w in range(CONV_WIDTH):
            acc = acc + u_buf[r0 + lead + w:r0 + lead + w + rs, :] * cw_ref[w:w + 1, :]
        mu = jnp.mean(acc, axis=-1, keepdims=True)
        xc = acc - mu
        var = jnp.mean(xc * xc, axis=-1, keepdims=True)
        y = xc * lax.rsqrt(var + EPS) * lg_ref[...] + lb_ref[...]
        c_buf[r0:r0 + rs, :] = (y * jax.nn.sigmoid(y)).astype(BF16)

    y = _dot(att_ref[...], wo_ref[0:D_ATT, :]) + _dot(c_buf[...], wo_ref[D_ATT:D_ATT + D_CONV, :])
    o_ref[...] = x_ref[...] + _rms(y, gp_ref[...])


def _mix_out(att, u, halo_src, x, conv_w, conv_b, cln_g, cln_b, w_out, g_post, *, tt,
             halo_from_u):
    b, t, d = x.shape
    n_t = t // tt
    rows = lambda c: pl.BlockSpec((None, tt, c), lambda i, j: (i, j, 0))
    if halo_from_u:
        per = tt // HALO
        halo_spec = pl.BlockSpec((None, HALO, D_CONV),
                                 lambda i, j: (i, jnp.maximum(j * per - 1, 0), 0))
    else:
        halo_spec = pl.BlockSpec((None, HALO, D_CONV), lambda i, j: (i, 0, 0))
    cw = jnp.pad(conv_w, ((0, HALO - CONV_WIDTH), (0, 0)))
    return pl.pallas_call(
        functools.partial(_mix_out_kernel, tt=tt, zero_first=halo_from_u),
        grid=(b, n_t),
        in_specs=[rows(D_ATT), rows(D_CONV), halo_spec, rows(d),
                  _const_spec(cw.shape), _const_spec((1, D_CONV)), _const_spec((1, D_CONV)),
                  _const_spec((1, D_CONV)), _const_spec(w_out.shape), _const_spec((1, d))],
        out_specs=rows(d),
        out_shape=jax.ShapeDtypeStruct((b, t, d), F32),
        scratch_shapes=[pltpu.VMEM((HALO + tt, D_CONV), F32), pltpu.VMEM((tt, D_CONV), BF16)],
        compiler_params=_params(2),
        name="mix_out",
    )(att, u, halo_src, x, cw, conv_b, cln_g, cln_b, w_out, g_post)


def _mem_attn_kernel(x_ref, mk_ref, mv_ref, gpre_ref, wq_ref, wo_ref, gpost_ref, o_ref, a_buf):
    x = x_ref[...]
    h = _rms(x, gpre_ref[...]).astype(BF16)
    q = (_dot(h, wq_ref[...]) * (MEM_HEAD_DIM ** -0.5)).astype(BF16)
    for hh in range(MEM_HEADS):
        sl = slice(hh * MEM_HEAD_DIM, (hh + 1) * MEM_HEAD_DIM)
        s = _dot_nt(q[:, sl], mk_ref[:, sl])
        m = jnp.max(s, axis=-1, keepdims=True)
        e = jnp.exp(s - m)
        inv_l = 1.0 / jnp.sum(e, axis=-1, keepdims=True)
        a_buf[:, sl] = (_dot(e.astype(BF16), mv_ref[:, sl]) * inv_l).astype(BF16)
    y = _dot(a_buf[...], wo_ref[...])
    o_ref[...] = x + _rms(y, gpost_ref[...])


def _mem_attn(x, mk, mv, g_pre, w_mq, w_mo, g_post, *, tt):
    b, t, d = x.shape
    n_t = t // tt
    rows = pl.BlockSpec((None, tt, d), lambda i, j: (i, j, 0))
    mem = pl.BlockSpec((None, N_MEM, d), lambda i, j: (i, 0, 0))
    return pl.pallas_call(
        _mem_attn_kernel,
        grid=(b, n_t),
        in_specs=[rows, mem, mem, _const_spec((1, d)), _const_spec((d, d)), _const_spec((d, d)),
                  _const_spec((1, d))],
        out_specs=rows,
        out_shape=jax.ShapeDtypeStruct((b, t, d), F32),
        scratch_shapes=[pltpu.VMEM((tt, d), BF16)],
        compiler_params=_params(2),
        name="mem_attn",
    )(x, mk, mv, g_pre, w_mq, w_mo, g_post)


def _ffn_kernel(x_ref, gpre_ref, wg_ref, wu_ref, wd_ref, gpost_ref, o_ref, acc_ref):
    x = x_ref[...]
    h = _rms(x, gpre_ref[...]).astype(BF16)
    for c in range(D_FF // FF_CHUNK):
        sl = slice(c * FF_CHUNK, (c + 1) * FF_CHUNK)
        gate = _dot(h, wg_ref[:, sl])
        up = _dot(h, wu_ref[:, sl])
        a = (gate * jax.nn.sigmoid(gate) * up).astype(BF16)
        part = _dot(a, wd_ref[sl, :])
        if c == 0:
            acc_ref[...] = part
        else:
            acc_ref[...] += part
    o_ref[...] = x + _rms(acc_ref[...], gpost_ref[...])


def _ffn(x, g_pre, w_gate, w_up, w_down, g_post, *, tt):
    n, d = x.shape
    rows = pl.BlockSpec((tt, d), lambda i: (i, 0))
    return pl.pallas_call(
        _ffn_kernel,
        grid=(n // tt,),
        in_specs=[rows, _const_spec((1, d)), _const_spec((d, D_FF)), _const_spec((d, D_FF)),
                  _const_spec((D_FF, d)), _const_spec((1, d))],
        out_specs=rows,
        out_shape=jax.ShapeDtypeStruct((n, d), F32),
        scratch_shapes=[pltpu.VMEM((tt, d), F32)],
        compiler_params=_params(1),
        name="ffn",
    )(x, g_pre, w_gate, w_up, w_down, g_post)


def kernel(x_prompt, x_sample, cache_att_k, cache_att_v, cache_conv, cache_mem_k, cache_mem_v, mem_prompt, g_mix_pre, g_mix_post, w_in, rel_bias, conv_w, conv_b, cln_g, cln_b, w_out, g_mem_pre, g_mem_post, g_mem_kv, w_mq, w_mk, w_mv, w_mo, g_ffn_pre, g_ffn_post, w_gate, w_up, w_down):
    depth = w_in.shape[0]
    b, t_p, d = x_prompt.shape
    b_s, t_s, _ = x_sample.shape
    past = cache_att_k.shape[2]
    assert d == D_MODEL and t_p % ATT_PAST == 0 and past == ATT_PAST
    assert t_s <= CHUNK and t_s % 16 == 0 and CONV_WIDTH - 1 <= HALO

    row = lambda a: a[None, :]
    xp, xs = x_prompt, x_sample
    outs = [[] for _ in range(8)]
    for l in range(depth):
        bf = lambda w: w[l].astype(BF16)
        w_in_l, w_out_l = bf(w_in), bf(w_out)
        w_mq_l, w_mk_l, w_mv_l, w_mo_l = bf(w_mq), bf(w_mk), bf(w_mv), bf(w_mo)
        w_gate_l, w_up_l, w_down_l = bf(w_gate), bf(w_up), bf(w_down)
        conv = (conv_w[l], row(conv_b[l]), row(cln_g[l]), row(cln_b[l]))

        mk_p, mv_p, mk_pb, mv_pb = _mem_kv(mem_prompt, row(g_mem_kv[l]), w_mk_l, w_mv_l)
        q, kt, v, u, k_last, v_last = _mixer_in_prompt(xp, row(g_mix_pre[l]), w_in_l)
        att = _band_attn_prompt(q, kt, v, _band_bias_prompt(rel_bias[l]))
        xp = _mix_out(att, u, u, xp, *conv, w_out_l, row(g_mix_post[l]),
                      tt=ATT_PAST // 2, halo_from_u=True)
        xp = _mem_attn(xp, mk_pb, mv_pb, row(g_mem_pre[l]), w_mq_l, w_mo_l, row(g_mem_post[l]),
                       tt=ATT_PAST)
        xp = _ffn(xp.reshape(b * t_p, d), row(g_ffn_pre[l]), w_gate_l, w_up_l, w_down_l,
                  row(g_ffn_post[l]), tt=ATT_PAST).reshape(b, t_p, d)
        outs[0].append(k_last.reshape(b, ATT_PAST, ATT_HEADS, ATT_HEAD_DIM))
        outs[1].append(v_last.reshape(b, ATT_PAST, ATT_HEADS, ATT_HEAD_DIM))
        outs[2].append(u[:, t_p - (CONV_WIDTH - 1):])
        outs[3].append(mk_p.reshape(b, N_MEM, MEM_HEADS, MEM_HEAD_DIM))
        outs[4].append(mv_p.reshape(b, N_MEM, MEM_HEADS, MEM_HEAD_DIM))

        q, k_new, v_new, u = _mixer_in_sample(xs.reshape(b_s * t_s, d), row(g_mix_pre[l]), w_in_l)
        u = u.reshape(b_s, t_s, D_CONV)
        att, nk, nv = _band_attn_sample(
            q.reshape(b_s, t_s, D_ATT), k_new.reshape(b_s, t_s, D_ATT),
            v_new.reshape(b_s, t_s, D_ATT), cache_att_k[l].reshape(b_s, past, D_ATT),
            cache_att_v[l].reshape(b_s, past, D_ATT), rel_bias[l])
        halo = jnp.pad(cache_conv[l], ((0, 0), (HALO - (CONV_WIDTH - 1), 0), (0, 0)))
        xs = _mix_out(att, u, halo, xs, *conv, w_out_l, row(g_mix_post[l]),
                      tt=t_s, halo_from_u=False)
        xs = _mem_attn(xs, cache_mem_k[l].reshape(b_s, N_MEM, d).astype(BF16),
                       cache_mem_v[l].reshape(b_s, N_MEM, d).astype(BF16),
                       row(g_mem_pre[l]), w_mq_l, w_mo_l, row(g_mem_post[l]), tt=t_s)
        xs = _ffn(xs.reshape(b_s * t_s, d), row(g_ffn_pre[l]), w_gate_l, w_up_l, w_down_l,
                  row(g_ffn_post[l]), tt=b_s * t_s).reshape(b_s, t_s, d)
        outs[5].append(nk.reshape(b_s, past, ATT_HEADS, ATT_HEAD_DIM))
        outs[6].append(nv.reshape(b_s, past, ATT_HEADS, ATT_HEAD_DIM))
        outs[7].append(jnp.concatenate([cache_conv[l], u], axis=1)[:, -(CONV_WIDTH - 1):])

    return (xp, xs) + tuple(jnp.stack(o, 0) for o in outs)
```

```python
import functools

import jax
import jax.numpy as jnp
from jax import lax
from jax.experimental import pallas as pl
from jax.experimental.pallas import tpu as pltpu

D_MODEL = 1024
CHUNK = 64
ATT_PAST = 512
BAND = ATT_PAST + CHUNK
D_ATT = 512
ATT_HEADS = 8
ATT_HEAD_DIM = 64
D_CONV = 512
CONV_WIDTH = 31
REL_CLIP = 128
N_MEM = 256
MEM_HEADS = 4
MEM_HEAD_DIM = 256
D_FF = 2816
EPS = 1e-6
NEG_INF = -1e30

LANES = 128
SUBLANES = 8
MXU_COLS = 256
GROUP = 4 * CHUNK
N_KEY_TILES = (ATT_PAST + GROUP) // MXU_COLS
KEY_WINDOW = BAND + CHUNK
HALO = 32
FF_CHUNK = MXU_COLS
VMEM_LIMIT = 56 * 1024 * 1024

F32 = jnp.float32
BF16 = jnp.bfloat16


def _params(n_grid):
    return pltpu.CompilerParams(dimension_semantics=("arbitrary",) * n_grid,
                                vmem_limit_bytes=VMEM_LIMIT)


def _const_spec(shape):
    zeros = (0,) * len(shape)
    return pl.BlockSpec(shape, lambda *_: zeros, pipeline_mode=pl.Buffered(1))


def _dot(a, b):
    return jnp.dot(a, b, preferred_element_type=F32)


def _dot_nt(a, b):
    return lax.dot_general(a, b, (((1,), (1,)), ((), ())), preferred_element_type=F32)


def _rms(x, g):
    y = x * lax.rsqrt(jnp.mean(x * x, axis=-1, keepdims=True) + EPS)
    return y * g


def _mem_kv_kernel(m_ref, g_ref, wk_ref, wv_ref, k_ref, v_ref, kb_ref, vb_ref):
    m = _rms(m_ref[...], g_ref[...]).astype(BF16)
    k = _dot(m, wk_ref[...])
    v = _dot(m, wv_ref[...])
    k_ref[...] = k
    v_ref[...] = v
    kb_ref[...] = k.astype(BF16)
    vb_ref[...] = v.astype(BF16)


def _mem_kv(mem, g, wk, wv):
    b, n, d = mem.shape
    row = pl.BlockSpec((None, n, d), lambda i: (i, 0, 0))
    return pl.pallas_call(
        _mem_kv_kernel,
        grid=(b,),
        in_specs=[row, _const_spec((1, d)), _const_spec((d, d)), _const_spec((d, d))],
        out_specs=[row, row, row, row],
        out_shape=[jax.ShapeDtypeStruct((b, n, d), F32)] * 2
        + [jax.ShapeDtypeStruct((b, n, d), BF16)] * 2,
        compiler_params=_params(1),
        name="mem_kv",
    )(mem, g, wk, wv)


def _mixer_proj(x_ref, g_ref, w_ref):
    h = _rms(x_ref[...], g_ref[...]).astype(BF16)
    q = _dot(h, w_ref[:, 0:D_ATT]) * (ATT_HEAD_DIM ** -0.5)
    k = _dot(h, w_ref[:, D_ATT:2 * D_ATT])
    v = _dot(h, w_ref[:, 2 * D_ATT:3 * D_ATT])
    u_val = _dot(h, w_ref[:, 3 * D_ATT:3 * D_ATT + D_CONV])
    u_gate = _dot(h, w_ref[:, 3 * D_ATT + D_CONV:3 * D_ATT + 2 * D_CONV])
    return q, k, v, u_val * jax.nn.sigmoid(u_gate)


def _mixer_in_prompt_kernel(x_ref, g_ref, w_ref, q_ref, kt_ref, v_ref, u_ref, kl_ref, vl_ref,
                            *, n_t):
    q, k, v, u = _mixer_proj(x_ref, g_ref, w_ref)
    q_ref[...] = q.astype(BF16)
    kt_ref[...] = k.T.astype(BF16)
    v_ref[...] = v.astype(BF16)
    u_ref[...] = u

    @pl.when(pl.program_id(1) == n_t - 1)
    def _():
        kl_ref[...] = k
        vl_ref[...] = v


def _mixer_in_prompt(x, g, w_in):
    b, t, d = x.shape
    tt = ATT_PAST
    n_t = t // tt
    d_in = w_in.shape[1]
    rows = lambda c: pl.BlockSpec((None, tt, c), lambda i, j: (i, j, 0))
    last = pl.BlockSpec((None, tt, D_ATT), lambda i, j: (i, 0, 0))
    return pl.pallas_call(
        functools.partial(_mixer_in_prompt_kernel, n_t=n_t),
        grid=(b, n_t),
        in_specs=[rows(d), _const_spec((1, d)), _const_spec((d, d_in))],
        out_specs=[rows(D_ATT),
                   pl.BlockSpec((None, D_ATT, tt), lambda i, j: (i, 0, j)),
                   rows(D_ATT), rows(D_CONV), last, last],
        out_shape=[jax.ShapeDtypeStruct((b, t, D_ATT), BF16),
                   jax.ShapeDtypeStruct((b, D_ATT, t), BF16),
                   jax.ShapeDtypeStruct((b, t, D_ATT), BF16),
                   jax.ShapeDtypeStruct((b, t, D_CONV), F32),
                   jax.ShapeDtypeStruct((b, tt, D_ATT), F32),
                   jax.ShapeDtypeStruct((b, tt, D_ATT), F32)],
        compiler_params=_params(2),
        name="mixer_in_prompt",
    )(x, g, w_in)


def _mixer_in_sample_kernel(x_ref, g_ref, w_ref, q_ref, k_ref, v_ref, u_ref):
    q, k, v, u = _mixer_proj(x_ref, g_ref, w_ref)
    q_ref[...] = q.astype(BF16)
    k_ref[...] = k
    v_ref[...] = v
    u_ref[...] = u


def _mixer_in_sample(x, g, w_in):
    n, d = x.shape
    d_in = w_in.shape[1]
    full = lambda c: pl.BlockSpec((n, c), lambda i: (0, 0))
    return pl.pallas_call(
        _mixer_in_sample_kernel,
        grid=(1,),
        in_specs=[full(d), _const_spec((1, d)), _const_spec((d, d_in))],
        out_specs=[full(D_ATT), full(D_ATT), full(D_ATT), full(D_CONV)],
        out_shape=[jax.ShapeDtypeStruct((n, D_ATT), BF16),
                   jax.ShapeDtypeStruct((n, D_ATT), F32),
                   jax.ShapeDtypeStruct((n, D_ATT), F32),
                   jax.ShapeDtypeStruct((n, D_CONV), F32)],
        compiler_params=_params(1),
        name="mixer_in_sample",
    )(x, g, w_in)


def _stack_head_pair(q2):
    lane = lax.broadcasted_iota(jnp.int32, q2.shape, 1)
    zero = jnp.zeros_like(q2)
    return jnp.concatenate([jnp.where(lane < ATT_HEAD_DIM, q2, zero),
                            jnp.where(lane >= ATT_HEAD_DIM, q2, zero)], axis=0)


def _merge_head_pair(o):
    m = o.shape[0] // 2
    lane = lax.broadcasted_iota(jnp.int32, (m, LANES), 1)
    return jnp.where(lane < ATT_HEAD_DIM, o[:m], o[m:])


def _band_attn_kernel(q_ref, kt0_ref, kt1_ref, kt2_ref, v0_ref, v1_ref, v2_ref, bias_ref,
                      o_ref, s_buf, p_buf):
    g = pl.program_id(1)
    kt_refs = (kt0_ref, kt1_ref, kt2_ref)
    v_refs = (v0_ref, v1_ref, v2_ref)
    n_chunks = GROUP // CHUNK
    for p in range(ATT_HEADS // 2):
        sl = slice(p * LANES, (p + 1) * LANES)
        qm = _stack_head_pair(q_ref[:, sl])
        for j in range(N_KEY_TILES):
            s = _dot(qm, kt_refs[j][sl, :])
            if j < N_KEY_TILES - 1:
                s = s + jnp.where(g - (N_KEY_TILES - 1) + j >= 0, 0.0, NEG_INF)
            s_buf[p, :, j * MXU_COLS:(j + 1) * MXU_COLS] = s
        inv_l = []
        for hh in range(2):
            for i in range(n_chunks):
                w0 = 0 if i < n_chunks // 2 else LANES
                r0 = hh * GROUP + i * CHUNK
                sb = s_buf[p, r0:r0 + CHUNK, w0:w0 + KEY_WINDOW] + bias_ref[2 * p + hh, i]
                m = jnp.max(sb, axis=-1, keepdims=True)
                e = jnp.exp(sb - m)
                inv_l.append(1.0 / jnp.sum(e, axis=-1, keepdims=True))
                p_buf[p, r0:r0 + CHUNK, w0:w0 + KEY_WINDOW] = e.astype(BF16)
                z0 = KEY_WINDOW if w0 == 0 else 0
                p_buf[p, r0:r0 + CHUNK, z0:z0 + LANES] = jnp.zeros((CHUNK, LANES), BF16)
        o = _dot(p_buf[p, :, 0:MXU_COLS], v_refs[0][:, sl])
        for j in range(1, N_KEY_TILES):
            o = o + _dot(p_buf[p, :, j * MXU_COLS:(j + 1) * MXU_COLS], v_refs[j][:, sl])
        o = o * jnp.concatenate(inv_l, axis=0)
        o_ref[:, sl] = _merge_head_pair(o).astype(BF16)


def _toeplitz_bias(rel_bias, n_rows, n_cols, shift):
    n_heads = rel_bias.shape[0]
    period = n_cols + n_rows
    e0 = shift - REL_CLIP
    pad = period + abs(e0)
    table = jnp.pad(rel_bias[:, ::-1].astype(F32), ((0, 0), (pad, pad)), mode="edge")
    base = pad - e0
    vec = jnp.concatenate([table[:, base:base + n_cols + 1],
                           table[:, base - (n_rows - 1):base]], axis=1)
    flat = jnp.tile(vec, (1, n_rows))[:, :n_rows * (period - 1)]
    return flat.reshape(n_heads, n_rows, period - 1)[:, :, :n_cols]


def _band_bias_prompt(rel_bias):
    n_chunks = GROUP // CHUNK
    tiles = []
    for i in range(n_chunks):
        w0 = 0 if i < n_chunks // 2 else LANES
        kj = w0 + jnp.arange(KEY_WINDOW) - CHUNK * i
        bias = _toeplitz_bias(rel_bias, CHUNK, KEY_WINDOW, ATT_PAST - w0 + CHUNK * i)
        tiles.append(jnp.where((kj >= 0) & (kj < BAND), bias, NEG_INF))
    return jnp.stack(tiles, axis=1)


def _band_attn_prompt(q, kt, v, bias):
    b, t, _ = q.shape
    n_g = t // GROUP
    back = N_KEY_TILES - 1
    kt_spec = lambda j: pl.BlockSpec(
        (None, D_ATT, MXU_COLS), lambda i, g: (i, 0, jnp.maximum(g - back + j, 0)))
    v_spec = lambda j: pl.BlockSpec(
        (None, MXU_COLS, D_ATT), lambda i, g: (i, jnp.maximum(g - back + j, 0), 0))
    rows = pl.BlockSpec((None, GROUP, D_ATT), lambda i, g: (i, g, 0))
    return pl.pallas_call(
        _band_attn_kernel,
        grid=(b, n_g),
        in_specs=[rows] + [kt_spec(j) for j in range(N_KEY_TILES)]
        + [v_spec(j) for j in range(N_KEY_TILES)] + [_const_spec(bias.shape)],
        out_specs=rows,
        out_shape=jax.ShapeDtypeStruct((b, t, D_ATT), BF16),
        scratch_shapes=[pltpu.VMEM((ATT_HEADS // 2, 2 * GROUP, N_KEY_TILES * MXU_COLS), F32),
                        pltpu.VMEM((ATT_HEADS // 2, 2 * GROUP, N_KEY_TILES * MXU_COLS), BF16)],
        compiler_params=_params(2),
        name="band_attn_prompt",
    )(q, kt, kt, kt, v, v, v, bias)


def _band_attn_sample_kernel(q_ref, kn_ref, vn_ref, ck_ref, cv_ref, bias_ref,
                             o_ref, nk_ref, nv_ref, k_buf, v_buf, *, t_s):
    past = ck_ref.shape[0]
    n_keys = k_buf.shape[0]
    nk_ref[0:past - t_s, :] = ck_ref[t_s:past, :]
    nk_ref[past - t_s:past, :] = kn_ref[...]
    nv_ref[0:past - t_s, :] = cv_ref[t_s:past, :]
    nv_ref[past - t_s:past, :] = vn_ref[...]

    k_buf[0:past, :] = ck_ref[...].astype(BF16)
    k_buf[past:past + t_s, :] = kn_ref[...].astype(BF16)
    k_buf[past + t_s:n_keys, :] = jnp.zeros((n_keys - past - t_s, D_ATT), BF16)
    v_buf[0:past, :] = cv_ref[...].astype(BF16)
    v_buf[past:past + t_s, :] = vn_ref[...].astype(BF16)
    v_buf[past + t_s:n_keys, :] = jnp.zeros((n_keys - past - t_s, D_ATT), BF16)

    for p in range(ATT_HEADS // 2):
        sl = slice(p * LANES, (p + 1) * LANES)
        qm = _stack_head_pair(q_ref[:, sl])
        bias = jnp.concatenate([bias_ref[2 * p], bias_ref[2 * p + 1]], axis=0)
        s = _dot_nt(qm, k_buf[:, sl]) + bias
        m = jnp.max(s, axis=-1, keepdims=True)
        e = jnp.exp(s - m)
        inv_l = 1.0 / jnp.sum(e, axis=-1, keepdims=True)
        o = _dot(e.astype(BF16), v_buf[:, sl]) * inv_l
        o_ref[:, sl] = _merge_head_pair(o).astype(BF16)


def _band_bias_sample(rel_bias, t_s, n_keys):
    bias = _toeplitz_bias(rel_bias, t_s, n_keys, ATT_PAST)
    return jnp.where(jnp.arange(n_keys) < ATT_PAST + t_s, bias, NEG_INF)


def _band_attn_sample(q, k_new, v_new, cache_k, cache_v, rel_bias):
    b, t_s, _ = q.shape
    past = cache_k.shape[1]
    n_keys = past + LANES
    bias = _band_bias_sample(rel_bias, t_s, n_keys)
    new = pl.BlockSpec((None, t_s, D_ATT), lambda i: (i, 0, 0))
    buf = pl.BlockSpec((None, past, D_ATT), lambda i: (i, 0, 0))
    return pl.pallas_call(
        functools.partial(_band_attn_sample_kernel, t_s=t_s),
        grid=(b,),
        in_specs=[new, new, new, buf, buf, _const_spec(bias.shape)],
        out_specs=[new, buf, buf],
        out_shape=[jax.ShapeDtypeStruct((b, t_s, D_ATT), BF16),
                   jax.ShapeDtypeStruct((b, past, D_ATT), F32),
                   jax.ShapeDtypeStruct((b, past, D_ATT), F32)],
        scratch_shapes=[pltpu.VMEM((n_keys, D_ATT), BF16), pltpu.VMEM((n_keys, D_ATT), BF16)],
        compiler_params=_params(1),
        name="band_attn_sample",
    )(q, k_new, v_new, cache_k, cache_v, bias)


def _mix_out_kernel(att_ref, u_ref, halo_ref, x_ref, cw_ref, cb_ref, lg_ref, lb_ref, wo_ref,
                    gp_ref, o_ref, u_buf, conv_buf, c_buf, *, tt, zero_first):
    halo = halo_ref[...]
    if zero_first:
        halo = jnp.where(pl.program_id(1) > 0, halo, 0.0)
    for c in range(D_CONV // LANES):
        sl = slice(c * LANES, (c + 1) * LANES)
        u_buf[c, 0:HALO, :] = halo[:, sl]
        u_buf[c, HALO:HALO + tt, :] = u_ref[:, sl]

    lead = HALO - (CONV_WIDTH - 1)
    for c in range(D_CONV // LANES):
        sl = slice(c * LANES, (c + 1) * LANES)
        for r0 in range(0, tt, SUBLANES):
            acc = jnp.zeros((SUBLANES, LANES), F32) + cb_ref[:, sl]
            for w in range(CONV_WIDTH):
                win = u_buf[c, pl.ds(r0 + lead + w, SUBLANES, stride=1), :]
                acc = acc + win * cw_ref[w:w + 1, sl]
            conv_buf[r0:r0 + SUBLANES, sl] = acc

    rs = min(tt, CHUNK)
    for r0 in range(0, tt, rs):
        acc = conv_buf[r0:r0 + rs, :]
        mu = jnp.mean(acc, axis=-1, keepdims=True)
        xc = acc - mu
        var = jnp.mean(xc * xc, axis=-1, keepdims=True)
        y = xc * lax.rsqrt(var + EPS) * lg_ref[...] + lb_ref[...]
        c_buf[r0:r0 + rs, :] = (y * jax.nn.sigmoid(y)).astype(BF16)

    y = _dot(att_ref[...], wo_ref[0:D_ATT, :]) + _dot(c_buf[...], wo_ref[D_ATT:D_ATT + D_CONV, :])
    o_ref[...] = x_ref[...] + _rms(y, gp_ref[...])


def _mix_out(att, u, halo_src, x, conv_w, conv_b, cln_g, cln_b, w_out, g_post, *, tt,
             halo_from_u):
    b, t, d = x.shape
    n_t = t // tt
    rows = lambda c: pl.BlockSpec((None, tt, c), lambda i, j: (i, j, 0))
    if halo_from_u:
        per = tt // HALO
        halo_spec = pl.BlockSpec((None, HALO, D_CONV),
                                 lambda i, j: (i, jnp.maximum(j * per - 1, 0), 0))
    else:
        halo_spec = pl.BlockSpec((None, HALO, D_CONV), lambda i, j: (i, 0, 0))
    cw = jnp.pad(conv_w, ((0, HALO - CONV_WIDTH), (0, 0)))
    return pl.pallas_call(
        functools.partial(_mix_out_kernel, tt=tt, zero_first=halo_from_u),
        grid=(b, n_t),
        in_specs=[rows(D_ATT), rows(D_CONV), halo_spec, rows(d),
                  _const_spec(cw.shape), _const_spec((1, D_CONV)), _const_spec((1, D_CONV)),
                  _const_spec((1, D_CONV)), _const_spec(w_out.shape), _const_spec((1, d))],
        out_specs=rows(d),
        out_shape=jax.ShapeDtypeStruct((b, t, d), F32),
        scratch_shapes=[pltpu.VMEM((D_CONV // LANES, HALO + tt, LANES), F32),
                        pltpu.VMEM((tt, D_CONV), F32), pltpu.VMEM((tt, D_CONV), BF16)],
        compiler_params=_params(2),
        name="mix_out",
    )(att, u, halo_src, x, cw, conv_b, cln_g, cln_b, w_out, g_post)


def _mem_attn_kernel(x_ref, mk_ref, mv_ref, gpre_ref, wq_ref, wo_ref, gpost_ref, o_ref, a_buf):
    x = x_ref[...]
    h = _rms(x, gpre_ref[...]).astype(BF16)
    q = (_dot(h, wq_ref[...]) * (MEM_HEAD_DIM ** -0.5)).astype(BF16)
    for hh in range(MEM_HEADS):
        sl = slice(hh * MEM_HEAD_DIM, (hh + 1) * MEM_HEAD_DIM)
        s = _dot_nt(q[:, sl], mk_ref[:, sl])
        m = jnp.max(s, axis=-1, keepdims=True)
        e = jnp.exp(s - m)
        inv_l = 1.0 / jnp.sum(e, axis=-1, keepdims=True)
        a_buf[:, sl] = (_dot(e.astype(BF16), mv_ref[:, sl]) * inv_l).astype(BF16)
    y = _dot(a_buf[...], wo_ref[...])
    o_ref[...] = x + _rms(y, gpost_ref[...])


def _mem_attn(x, mk, mv, g_pre, w_mq, w_mo, g_post, *, tt):
    b, t, d = x.shape
    n_t = t // tt
    rows = pl.BlockSpec((None, tt, d), lambda i, j: (i, j, 0))
    mem = pl.BlockSpec((None, N_MEM, d), lambda i, j: (i, 0, 0))
    return pl.pallas_call(
        _mem_attn_kernel,
        grid=(b, n_t),
        in_specs=[rows, mem, mem, _const_spec((1, d)), _const_spec((d, d)), _const_spec((d, d)),
                  _const_spec((1, d))],
        out_specs=rows,
        out_shape=jax.ShapeDtypeStruct((b, t, d), F32),
        scratch_shapes=[pltpu.VMEM((tt, d), BF16)],
        compiler_params=_params(2),
        name="mem_attn",
    )(x, mk, mv, g_pre, w_mq, w_mo, g_post)


def _ffn_kernel(x_ref, gpre_ref, wg_ref, wu_ref, wd_ref, gpost_ref, o_ref, acc_ref):
    x = x_ref[...]
    h = _rms(x, gpre_ref[...]).astype(BF16)
    for c in range(D_FF // FF_CHUNK):
        sl = slice(c * FF_CHUNK, (c + 1) * FF_CHUNK)
        gate = _dot(h, wg_ref[:, sl])
        up = _dot(h, wu_ref[:, sl])
        a = (gate * jax.nn.sigmoid(gate) * up).astype(BF16)
        part = _dot(a, wd_ref[sl, :])
        if c == 0:
            acc_ref[...] = part
        else:
            acc_ref[...] += part
    o_ref[...] = x + _rms(acc_ref[...], gpost_ref[...])


def _ffn(x, g_pre, w_gate, w_up, w_down, g_post, *, tt):
    n, d = x.shape
    rows = pl.BlockSpec((tt, d), lambda i: (i, 0))
    return pl.pallas_call(
        _ffn_kernel,
        grid=(n // tt,),
        in_specs=[rows, _const_spec((1, d)), _const_spec((d, D_FF)), _const_spec((d, D_FF)),
                  _const_spec((D_FF, d)), _const_spec((1, d))],
        out_specs=rows,
        out_shape=jax.ShapeDtypeStruct((n, d), F32),
        scratch_shapes=[pltpu.VMEM((tt, d), F32)],
        compiler_params=_params(1),
        name="ffn",
    )(x, g_pre, w_gate, w_up, w_down, g_post)


def kernel(x_prompt, x_sample, cache_att_k, cache_att_v, cache_conv, cache_mem_k, cache_mem_v, mem_prompt, g_mix_pre, g_mix_post, w_in, rel_bias, conv_w, conv_b, cln_g, cln_b, w_out, g_mem_pre, g_mem_post, g_mem_kv, w_mq, w_mk, w_mv, w_mo, g_ffn_pre, g_ffn_post, w_gate, w_up, w_down):
    depth = w_in.shape[0]
    b, t_p, d = x_prompt.shape
    b_s, t_s, _ = x_sample.shape
    past = cache_att_k.shape[2]
    assert d == D_MODEL and t_p % ATT_PAST == 0 and past == ATT_PAST
    assert t_s <= CHUNK and t_s % 16 == 0 and CONV_WIDTH - 1 <= HALO

    row = lambda a: a[None, :]
    xp, xs = x_prompt, x_sample
    outs = [[] for _ in range(8)]
    for l in range(depth):
        bf = lambda w: w[l].astype(BF16)
        w_in_l, w_out_l = bf(w_in), bf(w_out)
        w_mq_l, w_mk_l, w_mv_l, w_mo_l = bf(w_mq), bf(w_mk), bf(w_mv), bf(w_mo)
        w_gate_l, w_up_l, w_down_l = bf(w_gate), bf(w_up), bf(w_down)
        conv = (conv_w[l], row(conv_b[l]), row(cln_g[l]), row(cln_b[l]))

        mk_p, mv_p, mk_pb, mv_pb = _mem_kv(mem_prompt, row(g_mem_kv[l]), w_mk_l, w_mv_l)
        q, kt, v, u, k_last, v_last = _mixer_in_prompt(xp, row(g_mix_pre[l]), w_in_l)
        att = _band_attn_prompt(q, kt, v, _band_bias_prompt(rel_bias[l]))
        xp = _mix_out(att, u, u, xp, *conv, w_out_l, row(g_mix_post[l]),
                      tt=ATT_PAST // 2, halo_from_u=True)
        xp = _mem_attn(xp, mk_pb, mv_pb, row(g_mem_pre[l]), w_mq_l, w_mo_l, row(g_mem_post[l]),
                       tt=ATT_PAST)
        xp = _ffn(xp.reshape(b * t_p, d), row(g_ffn_pre[l]), w_gate_l, w_up_l, w_down_l,
                  row(g_ffn_post[l]), tt=ATT_PAST).reshape(b, t_p, d)
        outs[0].append(k_last.reshape(b, ATT_PAST, ATT_HEADS, ATT_HEAD_DIM))
        outs[1].append(v_last.reshape(b, ATT_PAST, ATT_HEADS, ATT_HEAD_DIM))
        outs[2].append(u[:, t_p - (CONV_WIDTH - 1):])
        outs[3].append(mk_p.reshape(b, N_MEM, MEM_HEADS, MEM_HEAD_DIM))
        outs[4].append(mv_p.reshape(b, N_MEM, MEM_HEADS, MEM_HEAD_DIM))

        q, k_new, v_new, u = _mixer_in_sample(xs.reshape(b_s * t_s, d), row(g_mix_pre[l]), w_in_l)
        u = u.reshape(b_s, t_s, D_CONV)
        att, nk, nv = _band_attn_sample(
            q.reshape(b_s, t_s, D_ATT), k_new.reshape(b_s, t_s, D_ATT),
            v_new.reshape(b_s, t_s, D_ATT), cache_att_k[l].reshape(b_s, past, D_ATT),
            cache_att_v[l].reshape(b_s, past, D_ATT), rel_bias[l])
        halo = jnp.pad(cache_conv[l], ((0, 0), (HALO - (CONV_WIDTH - 1), 0), (0, 0)))
        xs = _mix_out(att, u, halo, xs, *conv, w_out_l, row(g_mix_post[l]),
                      tt=t_s, halo_from_u=False)
        xs = _mem_attn(xs, cache_mem_k[l].reshape(b_s, N_MEM, d).astype(BF16),
                       cache_mem_v[l].reshape(b_s, N_MEM, d).astype(BF16),
                       row(g_mem_pre[l]), w_mq_l, w_mo_l, row(g_mem_post[l]), tt=t_s)
        xs = _ffn(xs.reshape(b_s * t_s, d), row(g_ffn_pre[l]), w_gate_l, w_up_l, w_down_l,
                  row(g_ffn_post[l]), tt=b_s * t_s).reshape(b_s, t_s, d)
        outs[5].append(nk.reshape(b_s, past, ATT_HEADS, ATT_HEAD_DIM))
        outs[6].append(nv.reshape(b_s, past, ATT_HEADS, ATT_HEAD_DIM))
        outs[7].append(jnp.concatenate([cache_conv[l], u], axis=1)[:, -(CONV_WIDTH - 1):])

    return (xp, xs) + tuple(jnp.stack(o, 0) for o in outs)
```

```python
import functools

import jax
import jax.numpy as jnp
from jax import lax
from jax.experimental import pallas as pl
from jax.experimental.pallas import tpu as pltpu

D_MODEL = 1024
CHUNK = 64
ATT_PAST = 512
BAND = ATT_PAST + CHUNK
D_ATT = 512
ATT_HEADS = 8
ATT_HEAD_DIM = 64
D_CONV = 512
CONV_WIDTH = 31
REL_CLIP = 128
N_MEM = 256
MEM_HEADS = 4
MEM_HEAD_DIM = 256
D_FF = 2816
EPS = 1e-6
NEG_INF = -1e30

LANES = 128
SUBLANES = 8
MXU_COLS = 256
GROUP = 4 * CHUNK
N_KEY_TILES = (ATT_PAST + GROUP) // MXU_COLS
ATT_STEP = ATT_PAST
LOG2E = 1.4426950408889634
Q_SCALE = ATT_HEAD_DIM ** -0.5 * LOG2E
KEY_WINDOW = BAND + CHUNK
HALO = 32
FF_CHUNK = MXU_COLS
VMEM_LIMIT = 56 * 1024 * 1024

F32 = jnp.float32
BF16 = jnp.bfloat16


def _params(n_grid):
    return pltpu.CompilerParams(dimension_semantics=("arbitrary",) * n_grid,
                                vmem_limit_bytes=VMEM_LIMIT)


def _const_spec(shape):
    zeros = (0,) * len(shape)
    return pl.BlockSpec(shape, lambda *_: zeros, pipeline_mode=pl.Buffered(1))


def _dot(a, b):
    return jnp.dot(a, b, preferred_element_type=F32)


def _dot_nt(a, b):
    return lax.dot_general(a, b, (((1,), (1,)), ((), ())), preferred_element_type=F32)


def _rms(x, g):
    y = x * lax.rsqrt(jnp.mean(x * x, axis=-1, keepdims=True) + EPS)
    return y * g


def _mem_kv_kernel(m_ref, g_ref, wk_ref, wv_ref, k_ref, v_ref, kb_ref, vb_ref):
    m = _rms(m_ref[...], g_ref[...]).astype(BF16)
    k = _dot(m, wk_ref[...])
    v = _dot(m, wv_ref[...])
    k_ref[...] = k
    v_ref[...] = v
    kb_ref[...] = k.astype(BF16)
    vb_ref[...] = v.astype(BF16)


def _mem_kv(mem, g, wk, wv):
    b, n, d = mem.shape
    row = pl.BlockSpec((None, n, d), lambda i: (i, 0, 0))
    return pl.pallas_call(
        _mem_kv_kernel,
        grid=(b,),
        in_specs=[row, _const_spec((1, d)), _const_spec((d, d)), _const_spec((d, d))],
        out_specs=[row, row, row, row],
        out_shape=[jax.ShapeDtypeStruct((b, n, d), F32)] * 2
        + [jax.ShapeDtypeStruct((b, n, d), BF16)] * 2,
        compiler_params=_params(1),
        name="mem_kv",
    )(mem, g, wk, wv)


def _mixer_proj(x_ref, g_ref, w_ref):
    h = _rms(x_ref[...], g_ref[...]).astype(BF16)
    q = _dot(h, w_ref[:, 0:D_ATT]) * Q_SCALE
    k = _dot(h, w_ref[:, D_ATT:2 * D_ATT])
    v = _dot(h, w_ref[:, 2 * D_ATT:3 * D_ATT])
    u_val = _dot(h, w_ref[:, 3 * D_ATT:3 * D_ATT + D_CONV])
    u_gate = _dot(h, w_ref[:, 3 * D_ATT + D_CONV:3 * D_ATT + 2 * D_CONV])
    return q, k, v, u_val * jax.nn.sigmoid(u_gate)


def _mixer_in_prompt_kernel(x_ref, g_ref, w_ref, qt_ref, k_ref, vt_ref, u_ref, kl_ref, vl_ref,
                            *, n_t):
    q, k, v, u = _mixer_proj(x_ref, g_ref, w_ref)
    qt_ref[...] = q.T.astype(BF16)
    k_ref[...] = k.astype(BF16)
    vt_ref[...] = v.T.astype(BF16)
    u_ref[...] = u

    @pl.when(pl.program_id(1) == n_t - 1)
    def _():
        kl_ref[...] = k
        vl_ref[...] = v


def _mixer_in_prompt(x, g, w_in):
    b, t, d = x.shape
    tt = ATT_PAST
    n_t = t // tt
    d_in = w_in.shape[1]
    rows = lambda c: pl.BlockSpec((None, tt, c), lambda i, j: (i, j, 0))
    last = pl.BlockSpec((None, tt, D_ATT), lambda i, j: (i, 0, 0))
    cols = pl.BlockSpec((None, D_ATT, tt), lambda i, j: (i, 0, j))
    return pl.pallas_call(
        functools.partial(_mixer_in_prompt_kernel, n_t=n_t),
        grid=(b, n_t),
        in_specs=[rows(d), _const_spec((1, d)), _const_spec((d, d_in))],
        out_specs=[cols, rows(D_ATT), cols, rows(D_CONV), last, last],
        out_shape=[jax.ShapeDtypeStruct((b, D_ATT, t), BF16),
                   jax.ShapeDtypeStruct((b, t, D_ATT), BF16),
                   jax.ShapeDtypeStruct((b, D_ATT, t), BF16),
                   jax.ShapeDtypeStruct((b, t, D_CONV), F32),
                   jax.ShapeDtypeStruct((b, tt, D_ATT), F32),
                   jax.ShapeDtypeStruct((b, tt, D_ATT), F32)],
        compiler_params=_params(2),
        name="mixer_in_prompt",
    )(x, g, w_in)


def _mixer_in_sample_kernel(x_ref, g_ref, w_ref, q_ref, k_ref, v_ref, u_ref):
    q, k, v, u = _mixer_proj(x_ref, g_ref, w_ref)
    q_ref[...] = q.astype(BF16)
    k_ref[...] = k
    v_ref[...] = v
    u_ref[...] = u


def _mixer_in_sample(x, g, w_in):
    n, d = x.shape
    d_in = w_in.shape[1]
    full = lambda c: pl.BlockSpec((n, c), lambda i: (0, 0))
    return pl.pallas_call(
        _mixer_in_sample_kernel,
        grid=(1,),
        in_specs=[full(d), _const_spec((1, d)), _const_spec((d, d_in))],
        out_specs=[full(D_ATT), full(D_ATT), full(D_ATT), full(D_CONV)],
        out_shape=[jax.ShapeDtypeStruct((n, D_ATT), BF16),
                   jax.ShapeDtypeStruct((n, D_ATT), F32),
                   jax.ShapeDtypeStruct((n, D_ATT), F32),
                   jax.ShapeDtypeStruct((n, D_CONV), F32)],
        compiler_params=_params(1),
        name="mixer_in_sample",
    )(x, g, w_in)


def _stack_head_pair(q2):
    lane = lax.broadcasted_iota(jnp.int32, q2.shape, 1)
    zero = jnp.zeros_like(q2)
    return jnp.concatenate([jnp.where(lane < ATT_HEAD_DIM, q2, zero),
                            jnp.where(lane >= ATT_HEAD_DIM, q2, zero)], axis=0)


def _merge_head_pair(o):
    m = o.shape[0] // 2
    lane = lax.broadcasted_iota(jnp.int32, (m, LANES), 1)
    return jnp.where(lane < ATT_HEAD_DIM, o[:m], o[m:])


def _band_attn_kernel(qt_ref, kp_ref, kc_ref, vtp_ref, vtc_ref, bias_ref, o_ref, s_buf, p_buf):
    first = pl.program_id(1) == 0
    row = lax.broadcasted_iota(jnp.int32, (LANES, GROUP), 0)
    items = [(grp, p) for grp in range(ATT_STEP // GROUP) for p in range(ATT_HEADS // 2)]
    tiles_per_block = ATT_STEP // MXU_COLS

    def tile_slice(tile):
        t = tile % tiles_per_block
        return slice(t * MXU_COLS, (t + 1) * MXU_COLS)

    def scores(n):
        grp, p = items[n]
        rows = slice(p * LANES, (p + 1) * LANES)
        qt2 = qt_ref[rows, grp * GROUP:(grp + 1) * GROUP]
        zero = jnp.zeros_like(qt2)
        qm = jnp.concatenate([jnp.where(row < ATT_HEAD_DIM, qt2, zero),
                              jnp.where(row >= ATT_HEAD_DIM, qt2, zero)], axis=1)
        for j in range(N_KEY_TILES):
            tile = grp + j
            k_ref = kp_ref if tile < tiles_per_block else kc_ref
            s_buf[n % 2, j * MXU_COLS:(j + 1) * MXU_COLS, :] = _dot(k_ref[tile_slice(tile), rows], qm)

    def finish(n):
        grp, p = items[n]
        slot = n % 2
        rows = slice(p * LANES, (p + 1) * LANES)
        variant = jnp.where(first, 1 + grp, 0)
        inv_l = []
        for lt in range(2 * GROUP // LANES):
            hh, cp = divmod(lt, GROUP // LANES)
            w0 = LANES * cp
            lanes = slice(lt * LANES, (lt + 1) * LANES)
            t = s_buf[slot, w0:w0 + KEY_WINDOW, lanes] + bias_ref[variant, 2 * p + hh, cp]
            m = jnp.max(t, axis=0, keepdims=True)
            e = jnp.exp2(t - m)
            inv_l.append(1.0 / jnp.sum(e, axis=0, keepdims=True))
            p_buf[slot, w0:w0 + KEY_WINDOW, lanes] = e.astype(BF16)
            z0 = KEY_WINDOW if cp == 0 else 0
            p_buf[slot, z0:z0 + LANES, lanes] = jnp.zeros((LANES, LANES), BF16)
        o = None
        for j in range(N_KEY_TILES):
            tile = grp + j
            vt_ref = vtp_ref if tile < tiles_per_block else vtc_ref
            d = _dot(vt_ref[rows, tile_slice(tile)], p_buf[slot, j * MXU_COLS:(j + 1) * MXU_COLS, :])
            o = d if o is None else o + d
        o = o * jnp.concatenate(inv_l, axis=1)
        ot = jnp.where(row < ATT_HEAD_DIM, o[:, 0:GROUP], o[:, GROUP:2 * GROUP])
        o_ref[grp * GROUP:(grp + 1) * GROUP, rows] = ot.T.astype(BF16)

    scores(0)
    for n in range(len(items)):
        if n + 1 < len(items):
            scores(n + 1)
        finish(n)


def _toeplitz_bias(rel_bias, n_rows, n_cols, shift):
    n_heads = rel_bias.shape[0]
    period = n_cols + n_rows
    e0 = shift - REL_CLIP
    pad = period + abs(e0)
    table = jnp.pad(rel_bias[:, ::-1].astype(F32), ((0, 0), (pad, pad)), mode="edge")
    base = pad - e0
    vec = jnp.concatenate([table[:, base:base + n_cols + 1],
                           table[:, base - (n_rows - 1):base]], axis=1)
    flat = jnp.tile(vec, (1, n_rows))[:, :n_rows * (period - 1)]
    return flat.reshape(n_heads, n_rows, period - 1)[:, :, :n_cols]


def _band_bias_prompt(rel_bias):
    variants = []
    for first_valid in (0, ATT_PAST, ATT_PAST - GROUP):
        windows = []
        for cp in range(GROUP // LANES):
            w0 = LANES * cp
            kk = w0 + jnp.arange(KEY_WINDOW)
            halves = []
            for i in (2 * cp, 2 * cp + 1):
                kj = kk - CHUNK * i
                bias = _toeplitz_bias(rel_bias, CHUNK, KEY_WINDOW, ATT_PAST - w0 + CHUNK * i) * LOG2E
                valid = (kj >= 0) & (kj < BAND) & (kk >= first_valid)
                halves.append(jnp.swapaxes(jnp.where(valid, bias, NEG_INF), 1, 2))
            windows.append(jnp.concatenate(halves, axis=2))
        variants.append(jnp.stack(windows, axis=1))
    return jnp.stack(variants, axis=0)


def _band_attn_prompt(qt, k, vt, bias):
    b, t, _ = k.shape
    prev = lambda g: jnp.maximum(g - 1, 0)
    cols = lambda f: pl.BlockSpec((None, D_ATT, ATT_STEP), lambda i, g: (i, 0, f(g)))
    rows = lambda f: pl.BlockSpec((None, ATT_STEP, D_ATT), lambda i, g: (i, f(g), 0))
    same = lambda g: g
    return pl.pallas_call(
        _band_attn_kernel,
        grid=(b, t // ATT_STEP),
        in_specs=[cols(same), rows(prev), rows(same), cols(prev), cols(same),
                  _const_spec(bias.shape)],
        out_specs=rows(same),
        out_shape=jax.ShapeDtypeStruct((b, t, D_ATT), BF16),
        scratch_shapes=[pltpu.VMEM((2, N_KEY_TILES * MXU_COLS, 2 * GROUP), F32),
                        pltpu.VMEM((2, N_KEY_TILES * MXU_COLS, 2 * GROUP), BF16)],
        compiler_params=_params(2),
        name="band_attn_prompt",
    )(qt, k, k, vt, vt, bias)


def _band_attn_sample_kernel(q_ref, kn_ref, vn_ref, ck_ref, cv_ref, bias_ref,
                             o_ref, nk_ref, nv_ref, k_buf, v_buf, *, t_s):
    past = ck_ref.shape[0]
    n_keys = k_buf.shape[0]
    nk_ref[0:past - t_s, :] = ck_ref[t_s:past, :]
    nk_ref[past - t_s:past, :] = kn_ref[...]
    nv_ref[0:past - t_s, :] = cv_ref[t_s:past, :]
    nv_ref[past - t_s:past, :] = vn_ref[...]

    k_buf[0:past, :] = ck_ref[...].astype(BF16)
    k_buf[past:past + t_s, :] = kn_ref[...].astype(BF16)
    k_buf[past + t_s:n_keys, :] = jnp.zeros((n_keys - past - t_s, D_ATT), BF16)
    v_buf[0:past, :] = cv_ref[...].astype(BF16)
    v_buf[past:past + t_s, :] = vn_ref[...].astype(BF16)
    v_buf[past + t_s:n_keys, :] = jnp.zeros((n_keys - past - t_s, D_ATT), BF16)

    for p in range(ATT_HEADS // 2):
        sl = slice(p * LANES, (p + 1) * LANES)
        qm = _stack_head_pair(q_ref[:, sl])
        bias = jnp.concatenate([bias_ref[2 * p], bias_ref[2 * p + 1]], axis=0)
        s = _dot_nt(qm, k_buf[:, sl]) + bias
        m = jnp.max(s, axis=-1, keepdims=True)
        e = jnp.exp2(s - m)
        inv_l = 1.0 / jnp.sum(e, axis=-1, keepdims=True)
        o = _dot(e.astype(BF16), v_buf[:, sl]) * inv_l
        o_ref[:, sl] = _merge_head_pair(o).astype(BF16)


def _band_bias_sample(rel_bias, t_s, n_keys):
    bias = _toeplitz_bias(rel_bias, t_s, n_keys, ATT_PAST) * LOG2E
    return jnp.where(jnp.arange(n_keys) < ATT_PAST + t_s, bias, NEG_INF)


def _band_attn_sample(q, k_new, v_new, cache_k, cache_v, rel_bias):
    b, t_s, _ = q.shape
    past = cache_k.shape[1]
    n_keys = past + LANES
    bias = _band_bias_sample(rel_bias, t_s, n_keys)
    new = pl.BlockSpec((None, t_s, D_ATT), lambda i: (i, 0, 0))
    buf = pl.BlockSpec((None, past, D_ATT), lambda i: (i, 0, 0))
    return pl.pallas_call(
        functools.partial(_band_attn_sample_kernel, t_s=t_s),
        grid=(b,),
        in_specs=[new, new, new, buf, buf, _const_spec(bias.shape)],
        out_specs=[new, buf, buf],
        out_shape=[jax.ShapeDtypeStruct((b, t_s, D_ATT), BF16),
                   jax.ShapeDtypeStruct((b, past, D_ATT), F32),
                   jax.ShapeDtypeStruct((b, past, D_ATT), F32)],
        scratch_shapes=[pltpu.VMEM((n_keys, D_ATT), BF16), pltpu.VMEM((n_keys, D_ATT), BF16)],
        compiler_params=_params(1),
        name="band_attn_sample",
    )(q, k_new, v_new, cache_k, cache_v, bias)


def _mix_out_kernel(att_ref, u_ref, halo_ref, x_ref, cw_ref, cb_ref, lg_ref, lb_ref, wo_ref,
                    gp_ref, o_ref, u_buf, conv_buf, c_buf, *, tt, zero_first):
    halo = halo_ref[...]
    if zero_first:
        halo = jnp.where(pl.program_id(1) > 0, halo, 0.0)
    for c in range(D_CONV // LANES):
        sl = slice(c * LANES, (c + 1) * LANES)
        u_buf[c, 0:HALO, :] = halo[:, sl]
        u_buf[c, HALO:HALO + tt, :] = u_ref[:, sl]

    lead = HALO - (CONV_WIDTH - 1)
    for c in range(D_CONV // LANES):
        sl = slice(c * LANES, (c + 1) * LANES)
        for r0 in range(0, tt, SUBLANES):
            acc = jnp.zeros((SUBLANES, LANES), F32) + cb_ref[:, sl]
            for w in range(CONV_WIDTH):
                win = u_buf[c, pl.ds(r0 + lead + w, SUBLANES, stride=1), :]
                acc = acc + win * cw_ref[w:w + 1, sl]
            conv_buf[r0:r0 + SUBLANES, sl] = acc

    rs = min(tt, CHUNK)
    for r0 in range(0, tt, rs):
        acc = conv_buf[r0:r0 + rs, :]
        mu = jnp.mean(acc, axis=-1, keepdims=True)
        xc = acc - mu
        var = jnp.mean(xc * xc, axis=-1, keepdims=True)
        y = xc * lax.rsqrt(var + EPS) * lg_ref[...] + lb_ref[...]
        c_buf[r0:r0 + rs, :] = (y * jax.nn.sigmoid(y)).astype(BF16)

    y = _dot(att_ref[...], wo_ref[0:D_ATT, :]) + _dot(c_buf[...], wo_ref[D_ATT:D_ATT + D_CONV, :])
    o_ref[...] = x_ref[...] + _rms(y, gp_ref[...])


def _mix_out(att, u, halo_src, x, conv_w, conv_b, cln_g, cln_b, w_out, g_post, *, tt,
             halo_from_u):
    b, t, d = x.shape
    n_t = t // tt
    rows = lambda c: pl.BlockSpec((None, tt, c), lambda i, j: (i, j, 0))
    if halo_from_u:
        per = tt // HALO
        halo_spec = pl.BlockSpec((None, HALO, D_CONV),
                                 lambda i, j: (i, jnp.maximum(j * per - 1, 0), 0))
    else:
        halo_spec = pl.BlockSpec((None, HALO, D_CONV), lambda i, j: (i, 0, 0))
    cw = jnp.pad(conv_w, ((0, HALO - CONV_WIDTH), (0, 0)))
    return pl.pallas_call(
        functools.partial(_mix_out_kernel, tt=tt, zero_first=halo_from_u),
        grid=(b, n_t),
        in_specs=[rows(D_ATT), rows(D_CONV), halo_spec, rows(d),
                  _const_spec(cw.shape), _const_spec((1, D_CONV)), _const_spec((1, D_CONV)),
                  _const_spec((1, D_CONV)), _const_spec(w_out.shape), _const_spec((1, d))],
        out_specs=rows(d),
        out_shape=jax.ShapeDtypeStruct((b, t, d), F32),
        scratch_shapes=[pltpu.VMEM((D_CONV // LANES, HALO + tt, LANES), F32),
                        pltpu.VMEM((tt, D_CONV), F32), pltpu.VMEM((tt, D_CONV), BF16)],
        compiler_params=_params(2),
        name="mix_out",
    )(att, u, halo_src, x, cw, conv_b, cln_g, cln_b, w_out, g_post)


def _mem_attn_kernel(x_ref, mk_ref, mv_ref, gpre_ref, wq_ref, wo_ref, gpost_ref, o_ref, a_buf):
    x = x_ref[...]
    h = _rms(x, gpre_ref[...]).astype(BF16)
    q = (_dot(h, wq_ref[...]) * (MEM_HEAD_DIM ** -0.5)).astype(BF16)
    for hh in range(MEM_HEADS):
        sl = slice(hh * MEM_HEAD_DIM, (hh + 1) * MEM_HEAD_DIM)
        s = _dot_nt(q[:, sl], mk_ref[:, sl])
        m = jnp.max(s, axis=-1, keepdims=True)
        e = jnp.exp(s - m)
        inv_l = 1.0 / jnp.sum(e, axis=-1, keepdims=True)
        a_buf[:, sl] = (_dot(e.astype(BF16), mv_ref[:, sl]) * inv_l).astype(BF16)
    y = _dot(a_buf[...], wo_ref[...])
    o_ref[...] = x + _rms(y, gpost_ref[...])


def _mem_attn(x, mk, mv, g_pre, w_mq, w_mo, g_post, *, tt):
    b, t, d = x.shape
    n_t = t // tt
    rows = pl.BlockSpec((None, tt, d), lambda i, j: (i, j, 0))
    mem = pl.BlockSpec((None, N_MEM, d), lambda i, j: (i, 0, 0))
    return pl.pallas_call(
        _mem_attn_kernel,
        grid=(b, n_t),
        in_specs=[rows, mem, mem, _const_spec((1, d)), _const_spec((d, d)), _const_spec((d, d)),
                  _const_spec((1, d))],
        out_specs=rows,
        out_shape=jax.ShapeDtypeStruct((b, t, d), F32),
        scratch_shapes=[pltpu.VMEM((tt, d), BF16)],
        compiler_params=_params(2),
        name="mem_attn",
    )(x, mk, mv, g_pre, w_mq, w_mo, g_post)


def _ffn_kernel(x_ref, gpre_ref, wg_ref, wu_ref, wd_ref, gpost_ref, o_ref, acc_ref):
    x = x_ref[...]
    h = _rms(x, gpre_ref[...]).astype(BF16)
    for c in range(D_FF // FF_CHUNK):
        sl = slice(c * FF_CHUNK, (c + 1) * FF_CHUNK)
        gate = _dot(h, wg_ref[:, sl])
        up = _dot(h, wu_ref[:, sl])
        a = (gate * jax.nn.sigmoid(gate) * up).astype(BF16)
        part = _dot(a, wd_ref[sl, :])
        if c == 0:
            acc_ref[...] = part
        else:
            acc_ref[...] += part
    o_ref[...] = x + _rms(acc_ref[...], gpost_ref[...])


def _ffn(x, g_pre, w_gate, w_up, w_down, g_post, *, tt):
    n, d = x.shape
    rows = pl.BlockSpec((tt, d), lambda i: (i, 0))
    return pl.pallas_call(
        _ffn_kernel,
        grid=(n // tt,),
        in_specs=[rows, _const_spec((1, d)), _const_spec((d, D_FF)), _const_spec((d, D_FF)),
                  _const_spec((D_FF, d)), _const_spec((1, d))],
        out_specs=rows,
        out_shape=jax.ShapeDtypeStruct((n, d), F32),
        scratch_shapes=[pltpu.VMEM((tt, d), F32)],
        compiler_params=_params(1),
        name="ffn",
    )(x, g_pre, w_gate, w_up, w_down, g_post)


def kernel(x_prompt, x_sample, cache_att_k, cache_att_v, cache_conv, cache_mem_k, cache_mem_v, mem_prompt, g_mix_pre, g_mix_post, w_in, rel_bias, conv_w, conv_b, cln_g, cln_b, w_out, g_mem_pre, g_mem_post, g_mem_kv, w_mq, w_mk, w_mv, w_mo, g_ffn_pre, g_ffn_post, w_gate, w_up, w_down):
    depth = w_in.shape[0]
    b, t_p, d = x_prompt.shape
    b_s, t_s, _ = x_sample.shape
    past = cache_att_k.shape[2]
    assert d == D_MODEL and t_p % ATT_PAST == 0 and past == ATT_PAST
    assert t_s <= CHUNK and t_s % 16 == 0 and CONV_WIDTH - 1 <= HALO

    row = lambda a: a[None, :]
    xp, xs = x_prompt, x_sample
    outs = [[] for _ in range(8)]
    for l in range(depth):
        bf = lambda w: w[l].astype(BF16)
        w_in_l, w_out_l = bf(w_in), bf(w_out)
        w_mq_l, w_mk_l, w_mv_l, w_mo_l = bf(w_mq), bf(w_mk), bf(w_mv), bf(w_mo)
        w_gate_l, w_up_l, w_down_l = bf(w_gate), bf(w_up), bf(w_down)
        conv = (conv_w[l], row(conv_b[l]), row(cln_g[l]), row(cln_b[l]))

        mk_p, mv_p, mk_pb, mv_pb = _mem_kv(mem_prompt, row(g_mem_kv[l]), w_mk_l, w_mv_l)
        qt, k, vt, u, k_last, v_last = _mixer_in_prompt(xp, row(g_mix_pre[l]), w_in_l)
        att = _band_attn_prompt(qt, k, vt, _band_bias_prompt(rel_bias[l]))
        xp = _mix_out(att, u, u, xp, *conv, w_out_l, row(g_mix_post[l]),
                      tt=ATT_PAST // 2, halo_from_u=True)
        xp = _mem_attn(xp, mk_pb, mv_pb, row(g_mem_pre[l]), w_mq_l, w_mo_l, row(g_mem_post[l]),
                       tt=ATT_PAST)
        xp = _ffn(xp.reshape(b * t_p, d), row(g_ffn_pre[l]), w_gate_l, w_up_l, w_down_l,
                  row(g_ffn_post[l]), tt=ATT_PAST).reshape(b, t_p, d)
        outs[0].append(k_last.reshape(b, ATT_PAST, ATT_HEADS, ATT_HEAD_DIM))
        outs[1].append(v_last.reshape(b, ATT_PAST, ATT_HEADS, ATT_HEAD_DIM))
        outs[2].append(u[:, t_p - (CONV_WIDTH - 1):])
        outs[3].append(mk_p.reshape(b, N_MEM, MEM_HEADS, MEM_HEAD_DIM))
        outs[4].append(mv_p.reshape(b, N_MEM, MEM_HEADS, MEM_HEAD_DIM))

        q, k_new, v_new, u = _mixer_in_sample(xs.reshape(b_s * t_s, d), row(g_mix_pre[l]), w_in_l)
        u = u.reshape(b_s, t_s, D_CONV)
        att, nk, nv = _band_attn_sample(
            q.reshape(b_s, t_s, D_ATT), k_new.reshape(b_s, t_s, D_ATT),
            v_new.reshape(b_s, t_s, D_ATT), cache_att_k[l].reshape(b_s, past, D_ATT),
            cache_att_v[l].reshape(b_s, past, D_ATT), rel_bias[l])
        halo = jnp.pad(cache_conv[l], ((0, 0), (HALO - (CONV_WIDTH - 1), 0), (0, 0)))
        xs = _mix_out(att, u, halo, xs, *conv, w_out_l, row(g_mix_post[l]),
                      tt=t_s, halo_from_u=False)
        xs = _mem_attn(xs, cache_mem_k[l].reshape(b_s, N_MEM, d).astype(BF16),
                       cache_mem_v[l].reshape(b_s, N_MEM, d).astype(BF16),
                       row(g_mem_pre[l]), w_mq_l, w_mo_l, row(g_mem_post[l]), tt=t_s)
        xs = _ffn(xs.reshape(b_s * t_s, d), row(g_ffn_pre[l]), w_gate_l, w_up_l, w_down_l,
                  row(g_ffn_post[l]), tt=b_s * t_s).reshape(b_s, t_s, d)
        outs[5].append(nk.reshape(b_s, past, ATT_HEADS, ATT_HEAD_DIM))
        outs[6].append(nv.reshape(b_s, past, ATT_HEADS, ATT_HEAD_DIM))
        outs[7].append(jnp.concatenate([cache_conv[l], u], axis=1)[:, -(CONV_WIDTH - 1):])

    return (xp, xs) + tuple(jnp.stack(o, 0) for o in outs)
```

```python
import functools

import jax
import jax.numpy as jnp
from jax import lax
from jax.experimental import pallas as pl
from jax.experimental.pallas import tpu as pltpu

D_MODEL = 1024
CHUNK = 64
ATT_PAST = 512
BAND = ATT_PAST + CHUNK
D_ATT = 512
ATT_HEADS = 8
ATT_HEAD_DIM = 64
D_CONV = 512
CONV_WIDTH = 31
REL_CLIP = 128
N_MEM = 256
MEM_HEADS = 4
MEM_HEAD_DIM = 256
D_FF = 2816
EPS = 1e-6
NEG_INF = -1e30

LANES = 128
SUBLANES = 8
MXU_COLS = 256
GROUP = 4 * CHUNK
N_KEY_TILES = (ATT_PAST + GROUP) // MXU_COLS
ATT_STEP = ATT_PAST
LOG2E = 1.4426950408889634
Q_SCALE = ATT_HEAD_DIM ** -0.5 * LOG2E
KEY_WINDOW = BAND + CHUNK
HALO = 32
FF_CHUNK = MXU_COLS
VMEM_LIMIT = 56 * 1024 * 1024

F32 = jnp.float32
BF16 = jnp.bfloat16


def _params(n_grid):
    return pltpu.CompilerParams(dimension_semantics=("arbitrary",) * n_grid,
                                vmem_limit_bytes=VMEM_LIMIT)


def _const_spec(shape):
    zeros = (0,) * len(shape)
    return pl.BlockSpec(shape, lambda *_: zeros, pipeline_mode=pl.Buffered(1))


def _dot(a, b):
    return jnp.dot(a, b, preferred_element_type=F32)


def _dot_nt(a, b):
    return lax.dot_general(a, b, (((1,), (1,)), ((), ())), preferred_element_type=F32)


def _rms(x, g):
    y = x * lax.rsqrt(jnp.mean(x * x, axis=-1, keepdims=True) + EPS)
    return y * g


def _mem_kv_kernel(m_ref, g_ref, wk_ref, wv_ref, k_ref, v_ref, kb_ref, vb_ref):
    m = _rms(m_ref[...], g_ref[...]).astype(BF16)
    k = _dot(m, wk_ref[...])
    v = _dot(m, wv_ref[...])
    k_ref[...] = k
    v_ref[...] = v
    kb_ref[...] = k.astype(BF16)
    vb_ref[...] = v.astype(BF16)


def _mem_kv(mem, g, wk, wv):
    b, n, d = mem.shape
    row = pl.BlockSpec((None, n, d), lambda i: (i, 0, 0))
    return pl.pallas_call(
        _mem_kv_kernel,
        grid=(b,),
        in_specs=[row, _const_spec((1, d)), _const_spec((d, d)), _const_spec((d, d))],
        out_specs=[row, row, row, row],
        out_shape=[jax.ShapeDtypeStruct((b, n, d), F32)] * 2
        + [jax.ShapeDtypeStruct((b, n, d), BF16)] * 2,
        compiler_params=_params(1),
        name="mem_kv",
    )(mem, g, wk, wv)


def _mixer_proj(x_ref, g_ref, w_ref):
    h = _rms(x_ref[...], g_ref[...]).astype(BF16)
    edges = (0, D_ATT, 2 * D_ATT, 3 * D_ATT, 3 * D_ATT + D_CONV, 3 * D_ATT + 2 * D_CONV)
    raw = [_dot(h, w_ref[:, a:b]) for a, b in zip(edges[:-1], edges[1:])]
    q, k, v, u_val, u_gate = raw
    return q * Q_SCALE, k, v, u_val * jax.nn.sigmoid(u_gate), raw


def _zero_after(x):
    bits = lax.bitcast_convert_type(x, jnp.uint32)
    zero = lax.shift_right_logical(lax.shift_right_logical(bits, jnp.uint32(16)), jnp.uint32(16))
    return lax.bitcast_convert_type(zero, F32)


def _conv_module(window, cw_ref, cb_ref, lg_ref, lb_ref, conv_buf, out_ref, tt, gate=None):
    lead = HALO - (CONV_WIDTH - 1)
    rs = min(tt, CHUNK)
    blocks = rs // SUBLANES
    strips = D_CONV // LANES
    for r0 in range(0, tt, rs):
        for c in range(strips):
            sl = slice(c * LANES, (c + 1) * LANES)
            taps = [jnp.broadcast_to(cw_ref[w:w + 1, sl], (SUBLANES, LANES))
                    for w in range(CONV_WIDTH)]
            bias = jnp.broadcast_to(cb_ref[:, sl], (SUBLANES, LANES))
            if gate is not None:
                bias = bias + gate((r0 // rs) * strips + c)
            accs = [bias] * blocks
            for d in range(SUBLANES * (blocks - 1) + CONV_WIDTH):
                win = window(c, r0 + lead + d)
                for b in range(blocks):
                    w = d - SUBLANES * b
                    if 0 <= w < CONV_WIDTH:
                        accs[b] = accs[b] + win * taps[w]
            for b in range(blocks):
                conv_buf[r0 + b * SUBLANES:r0 + (b + 1) * SUBLANES, sl] = accs[b]

        acc = conv_buf[r0:r0 + rs, :]
        mu = jnp.mean(acc, axis=-1, keepdims=True)
        xc = acc - mu
        var = jnp.mean(xc * xc, axis=-1, keepdims=True)
        y = xc * lax.rsqrt(var + EPS) * lg_ref[...] + lb_ref[...]
        out_ref[r0:r0 + rs, :] = (y * jax.nn.sigmoid(y)).astype(BF16)


def _mixer_in_prompt_kernel(x_ref, g_ref, w_ref, cw_ref, cb_ref, lg_ref, lb_ref,
                            qt_ref, k_ref, vt_ref, c_ref, kl_ref, vl_ref, ul_ref,
                            u_buf, u_next, conv_buf, *, tt):
    strips = D_CONV // LANES

    @pl.when(pl.program_id(1) == 0)
    def _():
        u_buf[...] = jnp.zeros(u_buf.shape, F32)

    q, k, v, u, raw = _mixer_proj(x_ref, g_ref, w_ref)
    qt_ref[...] = q.T.astype(BF16)
    k_ref[...] = k.astype(BF16)
    vt_ref[...] = v.T.astype(BF16)
    kl_ref[...] = k
    vl_ref[...] = v
    ul_ref[...] = u[tt - HALO:tt, :]
    u_next[...] = u

    n_units = (tt // CHUNK) * strips
    col_tiles = D_ATT // MXU_COLS
    n_pass = len(raw) * col_tiles

    def gate(n):
        pos = (n + 1) * n_pass / (n_units + 1)
        dot_i, nt = divmod(int(pos), col_tiles)
        r = int((pos - int(pos)) * (tt // SUBLANES)) * SUBLANES
        return _zero_after(raw[dot_i][r:r + SUBLANES, nt * MXU_COLS:nt * MXU_COLS + LANES])

    window = lambda c, start: u_buf[c, pl.ds(start, SUBLANES, stride=1), :]
    _conv_module(window, cw_ref, cb_ref, lg_ref, lb_ref, conv_buf, c_ref, tt, gate=gate)

    for c in range(strips):
        u_buf[c, 0:HALO, :] = u_buf[c, tt:tt + HALO, :]
        u_buf[c, HALO:HALO + tt, :] = u_next[:, c * LANES:(c + 1) * LANES]


def _mixer_in_prompt(x, g, w_in, conv_w, conv_b, cln_g, cln_b):
    b, t, d = x.shape
    tt = ATT_PAST
    n_t = t // tt
    d_in = w_in.shape[1]
    cur = lambda j: jnp.minimum(j, n_t - 1)
    lag = lambda j: jnp.maximum(j - 1, 0)
    rows = lambda c, f: pl.BlockSpec((None, tt, c), lambda i, j: (i, f(j), 0))
    cols = pl.BlockSpec((None, D_ATT, tt), lambda i, j: (i, 0, cur(j)))
    last = lambda r: pl.BlockSpec((None, r, D_ATT), lambda i, j: (i, 0, 0))
    return pl.pallas_call(
        functools.partial(_mixer_in_prompt_kernel, tt=tt),
        grid=(b, n_t + 1),
        in_specs=[rows(d, cur), _const_spec((1, d)), _const_spec((d, d_in)),
                  _const_spec(conv_w.shape), _const_spec((1, D_CONV)), _const_spec((1, D_CONV)),
                  _const_spec((1, D_CONV))],
        out_specs=[cols, rows(D_ATT, cur), cols, rows(D_CONV, lag), last(tt), last(tt),
                   last(HALO)],
        out_shape=[jax.ShapeDtypeStruct((b, D_ATT, t), BF16),
                   jax.ShapeDtypeStruct((b, t, D_ATT), BF16),
                   jax.ShapeDtypeStruct((b, D_ATT, t), BF16),
                   jax.ShapeDtypeStruct((b, t, D_CONV), BF16),
                   jax.ShapeDtypeStruct((b, tt, D_ATT), F32),
                   jax.ShapeDtypeStruct((b, tt, D_ATT), F32),
                   jax.ShapeDtypeStruct((b, HALO, D_CONV), F32)],
        scratch_shapes=[pltpu.VMEM((D_CONV // LANES, HALO + tt, LANES), F32),
                        pltpu.VMEM((tt, D_CONV), F32), pltpu.VMEM((tt, D_CONV), F32)],
        compiler_params=_params(2),
        name="mixer_in_prompt",
    )(x, g, w_in, conv_w, conv_b, cln_g, cln_b)


def _mixer_in_sample_kernel(x_ref, g_ref, w_ref, q_ref, k_ref, v_ref, u_ref):
    q, k, v, u, _ = _mixer_proj(x_ref, g_ref, w_ref)
    q_ref[...] = q.astype(BF16)
    k_ref[...] = k
    v_ref[...] = v
    u_ref[...] = u


def _mixer_in_sample(x, g, w_in):
    n, d = x.shape
    d_in = w_in.shape[1]
    full = lambda c: pl.BlockSpec((n, c), lambda i: (0, 0))
    return pl.pallas_call(
        _mixer_in_sample_kernel,
        grid=(1,),
        in_specs=[full(d), _const_spec((1, d)), _const_spec((d, d_in))],
        out_specs=[full(D_ATT), full(D_ATT), full(D_ATT), full(D_CONV)],
        out_shape=[jax.ShapeDtypeStruct((n, D_ATT), BF16),
                   jax.ShapeDtypeStruct((n, D_ATT), F32),
                   jax.ShapeDtypeStruct((n, D_ATT), F32),
                   jax.ShapeDtypeStruct((n, D_CONV), F32)],
        compiler_params=_params(1),
        name="mixer_in_sample",
    )(x, g, w_in)


def _stack_head_pair(q2):
    lane = lax.broadcasted_iota(jnp.int32, q2.shape, 1)
    zero = jnp.zeros_like(q2)
    return jnp.concatenate([jnp.where(lane < ATT_HEAD_DIM, q2, zero),
                            jnp.where(lane >= ATT_HEAD_DIM, q2, zero)], axis=0)


def _merge_head_pair(o):
    m = o.shape[0] // 2
    lane = lax.broadcasted_iota(jnp.int32, (m, LANES), 1)
    return jnp.where(lane < ATT_HEAD_DIM, o[:m], o[m:])


def _band_attn_kernel(qt_ref, kp_ref, kc_ref, vtp_ref, vtc_ref, bias_ref, o_ref, s_buf, p_buf):
    first = pl.program_id(1) == 0
    row = lax.broadcasted_iota(jnp.int32, (LANES, GROUP), 0)
    items = [(grp, p) for grp in range(ATT_STEP // GROUP) for p in range(ATT_HEADS // 2)]
    tiles_per_block = ATT_STEP // MXU_COLS
    n_lt = 2 * GROUP // LANES

    def tile_slice(tile):
        t = tile % tiles_per_block
        return slice(t * MXU_COLS, (t + 1) * MXU_COLS)

    def scores(n):
        grp, p = items[n]
        rows = slice(p * LANES, (p + 1) * LANES)
        qt2 = qt_ref[rows, grp * GROUP:(grp + 1) * GROUP]
        zero = jnp.zeros_like(qt2)
        qm = jnp.concatenate([jnp.where(row < ATT_HEAD_DIM, qt2, zero),
                              jnp.where(row >= ATT_HEAD_DIM, qt2, zero)], axis=1)
        for j in range(N_KEY_TILES):
            tile = grp + j
            k_ref = kp_ref if tile < tiles_per_block else kc_ref
            s = _dot(k_ref[tile_slice(tile), rows], qm)
            for lt in range(n_lt):
                s_buf[n % 2, lt, j * MXU_COLS:(j + 1) * MXU_COLS, :] = s[:, lt * LANES:(lt + 1) * LANES]

    def finish(n):
        grp, p = items[n]
        slot = n % 2
        rows = slice(p * LANES, (p + 1) * LANES)
        variant = jnp.where(first, 1 + grp, 0)
        inv_l = []
        for lt in range(n_lt):
            hh, cp = divmod(lt, GROUP // LANES)
            w0 = LANES * cp
            t = s_buf[slot, lt, w0:w0 + KEY_WINDOW, :] + bias_ref[variant, 2 * p + hh, cp]
            m = jnp.max(t, axis=0, keepdims=True)
            e = jnp.exp2(t - m)
            inv_l.append(1.0 / jnp.sum(e, axis=0, keepdims=True))
            p_buf[slot, lt, w0:w0 + KEY_WINDOW, :] = e.astype(BF16)
            z0 = KEY_WINDOW if cp == 0 else 0
            p_buf[slot, lt, z0:z0 + LANES, :] = jnp.zeros((LANES, LANES), BF16)
        o = None
        for j in range(N_KEY_TILES):
            tile = grp + j
            vt_ref = vtp_ref if tile < tiles_per_block else vtc_ref
            pt = jnp.concatenate([p_buf[slot, lt, j * MXU_COLS:(j + 1) * MXU_COLS, :]
                                  for lt in range(n_lt)], axis=1)
            d = _dot(vt_ref[rows, tile_slice(tile)], pt)
            o = d if o is None else o + d
        o = o * jnp.concatenate(inv_l, axis=1)
        ot = jnp.where(row < ATT_HEAD_DIM, o[:, 0:GROUP], o[:, GROUP:2 * GROUP])
        o_ref[grp * GROUP:(grp + 1) * GROUP, rows] = ot.T.astype(BF16)

    scores(0)
    for n in range(len(items)):
        if n + 1 < len(items):
            scores(n + 1)
        finish(n)


def _toeplitz_bias(rel_bias, n_rows, n_cols, shift):
    n_heads = rel_bias.shape[0]
    period = n_cols + n_rows
    e0 = shift - REL_CLIP
    pad = period + abs(e0)
    table = jnp.pad(rel_bias[:, ::-1].astype(F32), ((0, 0), (pad, pad)), mode="edge")
    base = pad - e0
    vec = jnp.concatenate([table[:, base:base + n_cols + 1],
                           table[:, base - (n_rows - 1):base]], axis=1)
    flat = jnp.tile(vec, (1, n_rows))[:, :n_rows * (period - 1)]
    return flat.reshape(n_heads, n_rows, period - 1)[:, :, :n_cols]


def _band_bias_prompt(rel_bias):
    variants = []
    for first_valid in (0, ATT_PAST, ATT_PAST - GROUP):
        windows = []
        for cp in range(GROUP // LANES):
            w0 = LANES * cp
            kk = w0 + jnp.arange(KEY_WINDOW)
            halves = []
            for i in (2 * cp, 2 * cp + 1):
                kj = kk - CHUNK * i
                bias = _toeplitz_bias(rel_bias, CHUNK, KEY_WINDOW, ATT_PAST - w0 + CHUNK * i) * LOG2E
                valid = (kj >= 0) & (kj < BAND) & (kk >= first_valid)
                halves.append(jnp.swapaxes(jnp.where(valid, bias, NEG_INF), 1, 2))
            windows.append(jnp.concatenate(halves, axis=2))
        variants.append(jnp.stack(windows, axis=1))
    return jnp.stack(variants, axis=0)


def _band_attn_prompt(qt, k, vt, bias):
    b, t, _ = k.shape
    prev = lambda g: jnp.maximum(g - 1, 0)
    cols = lambda f: pl.BlockSpec((None, D_ATT, ATT_STEP), lambda i, g: (i, 0, f(g)))
    rows = lambda f: pl.BlockSpec((None, ATT_STEP, D_ATT), lambda i, g: (i, f(g), 0))
    same = lambda g: g
    return pl.pallas_call(
        _band_attn_kernel,
        grid=(b, t // ATT_STEP),
        in_specs=[cols(same), rows(prev), rows(same), cols(prev), cols(same),
                  _const_spec(bias.shape)],
        out_specs=rows(same),
        out_shape=jax.ShapeDtypeStruct((b, t, D_ATT), BF16),
        scratch_shapes=[pltpu.VMEM((2, 2 * GROUP // LANES, N_KEY_TILES * MXU_COLS, LANES), F32),
                        pltpu.VMEM((2, 2 * GROUP // LANES, N_KEY_TILES * MXU_COLS, LANES), BF16)],
        compiler_params=_params(2),
        name="band_attn_prompt",
    )(qt, k, k, vt, vt, bias)


def _band_attn_sample_kernel(q_ref, kn_ref, vn_ref, ck_ref, cv_ref, bias_ref,
                             o_ref, nk_ref, nv_ref, k_buf, v_buf, *, t_s):
    past = ck_ref.shape[0]
    n_keys = k_buf.shape[0]
    nk_ref[0:past - t_s, :] = ck_ref[t_s:past, :]
    nk_ref[past - t_s:past, :] = kn_ref[...]
    nv_ref[0:past - t_s, :] = cv_ref[t_s:past, :]
    nv_ref[past - t_s:past, :] = vn_ref[...]

    k_buf[0:past, :] = ck_ref[...].astype(BF16)
    k_buf[past:past + t_s, :] = kn_ref[...].astype(BF16)
    k_buf[past + t_s:n_keys, :] = jnp.zeros((n_keys - past - t_s, D_ATT), BF16)
    v_buf[0:past, :] = cv_ref[...].astype(BF16)
    v_buf[past:past + t_s, :] = vn_ref[...].astype(BF16)
    v_buf[past + t_s:n_keys, :] = jnp.zeros((n_keys - past - t_s, D_ATT), BF16)

    for p in range(ATT_HEADS // 2):
        sl = slice(p * LANES, (p + 1) * LANES)
        qm = _stack_head_pair(q_ref[:, sl])
        bias = jnp.concatenate([bias_ref[2 * p], bias_ref[2 * p + 1]], axis=0)
        s = _dot_nt(qm, k_buf[:, sl]) + bias
        m = jnp.max(s, axis=-1, keepdims=True)
        e = jnp.exp2(s - m)
        inv_l = 1.0 / jnp.sum(e, axis=-1, keepdims=True)
        o = _dot(e.astype(BF16), v_buf[:, sl]) * inv_l
        o_ref[:, sl] = _merge_head_pair(o).astype(BF16)


def _band_bias_sample(rel_bias, t_s, n_keys):
    bias = _toeplitz_bias(rel_bias, t_s, n_keys, ATT_PAST) * LOG2E
    return jnp.where(jnp.arange(n_keys) < ATT_PAST + t_s, bias, NEG_INF)


def _band_attn_sample(q, k_new, v_new, cache_k, cache_v, rel_bias):
    b, t_s, _ = q.shape
    past = cache_k.shape[1]
    n_keys = past + LANES
    bias = _band_bias_sample(rel_bias, t_s, n_keys)
    new = pl.BlockSpec((None, t_s, D_ATT), lambda i: (i, 0, 0))
    buf = pl.BlockSpec((None, past, D_ATT), lambda i: (i, 0, 0))
    return pl.pallas_call(
        functools.partial(_band_attn_sample_kernel, t_s=t_s),
        grid=(b,),
        in_specs=[new, new, new, buf, buf, _const_spec(bias.shape)],
        out_specs=[new, buf, buf],
        out_shape=[jax.ShapeDtypeStruct((b, t_s, D_ATT), BF16),
                   jax.ShapeDtypeStruct((b, past, D_ATT), F32),
                   jax.ShapeDtypeStruct((b, past, D_ATT), F32)],
        scratch_shapes=[pltpu.VMEM((n_keys, D_ATT), BF16), pltpu.VMEM((n_keys, D_ATT), BF16)],
        compiler_params=_params(1),
        name="band_attn_sample",
    )(q, k_new, v_new, cache_k, cache_v, bias)


def _out_proj_residual(att, c, x, wo_ref, gp_ref):
    y = _dot(att, wo_ref[0:D_ATT, :]) + _dot(c, wo_ref[D_ATT:D_ATT + D_CONV, :])
    return x + _rms(y, gp_ref[...])


def _mix_out_sample_kernel(att_ref, u_ref, halo_ref, x_ref, cw_ref, cb_ref, lg_ref, lb_ref,
                           wo_ref, gp_ref, o_ref, u_buf, conv_buf, c_buf, *, tt):
    for c in range(D_CONV // LANES):
        sl = slice(c * LANES, (c + 1) * LANES)
        u_buf[c, 0:HALO, :] = halo_ref[:, sl]
        u_buf[c, HALO:HALO + tt, :] = u_ref[:, sl]
    window = lambda c, start: u_buf[c, pl.ds(start, SUBLANES, stride=1), :]
    _conv_module(window, cw_ref, cb_ref, lg_ref, lb_ref, conv_buf, c_buf, tt)
    o_ref[...] = _out_proj_residual(att_ref[...], c_buf[...], x_ref[...], wo_ref, gp_ref)


def _mix_out_sample(att, u, halo, x, conv_w, conv_b, cln_g, cln_b, w_out, g_post):
    b, tt, d = x.shape
    rows = lambda r, c: pl.BlockSpec((None, r, c), lambda i: (i, 0, 0))
    return pl.pallas_call(
        functools.partial(_mix_out_sample_kernel, tt=tt),
        grid=(b,),
        in_specs=[rows(tt, D_ATT), rows(tt, D_CONV), rows(HALO, D_CONV), rows(tt, d),
                  _const_spec(conv_w.shape), _const_spec((1, D_CONV)), _const_spec((1, D_CONV)),
                  _const_spec((1, D_CONV)), _const_spec(w_out.shape), _const_spec((1, d))],
        out_specs=rows(tt, d),
        out_shape=jax.ShapeDtypeStruct((b, tt, d), F32),
        scratch_shapes=[pltpu.VMEM((D_CONV // LANES, HALO + tt, LANES), F32),
                        pltpu.VMEM((tt, D_CONV), F32), pltpu.VMEM((tt, D_CONV), BF16)],
        compiler_params=_params(1),
        name="mix_out_sample",
    )(att, u, halo, x, conv_w, conv_b, cln_g, cln_b, w_out, g_post)


def _mix_mem_attn_kernel(att_ref, c_ref, wout_ref, gmix_ref, *rest):
    x_ref = rest[0]
    x = _out_proj_residual(att_ref[...], c_ref[...], x_ref[...], wout_ref, gmix_ref)
    _mem_attn_body(x, *rest[1:])


def _mem_attn_kernel(x_ref, *rest):
    _mem_attn_body(x_ref[...], *rest)


def _mem_attn_body(x, mk_ref, mv_ref, gpre_ref, wq_ref, wo_ref, gpost_ref, o_ref, a_buf):
    h = _rms(x, gpre_ref[...]).astype(BF16)
    q = (_dot(h, wq_ref[...]) * (MEM_HEAD_DIM ** -0.5)).astype(BF16)
    for hh in range(MEM_HEADS):
        sl = slice(hh * MEM_HEAD_DIM, (hh + 1) * MEM_HEAD_DIM)
        s = _dot_nt(q[:, sl], mk_ref[:, sl])
        m = jnp.max(s, axis=-1, keepdims=True)
        e = jnp.exp(s - m)
        inv_l = 1.0 / jnp.sum(e, axis=-1, keepdims=True)
        a_buf[:, sl] = (_dot(e.astype(BF16), mv_ref[:, sl]) * inv_l).astype(BF16)
    y = _dot(a_buf[...], wo_ref[...])
    o_ref[...] = x + _rms(y, gpost_ref[...])


def _mem_attn(x, mk, mv, g_pre, w_mq, w_mo, g_post, *, tt, mix=None):
    b, t, d = x.shape
    n_t = t // tt
    rows = lambda c: pl.BlockSpec((None, tt, c), lambda i, j: (i, j, 0))
    mem = pl.BlockSpec((None, N_MEM, d), lambda i, j: (i, 0, 0))
    in_specs = [rows(d), mem, mem, _const_spec((1, d)), _const_spec((d, d)), _const_spec((d, d)),
                _const_spec((1, d))]
    args = (x, mk, mv, g_pre, w_mq, w_mo, g_post)
    body = _mem_attn_kernel
    if mix is not None:
        in_specs = [rows(D_ATT), rows(D_CONV), _const_spec(mix[2].shape), _const_spec((1, d))] + in_specs
        args = tuple(mix) + args
        body = _mix_mem_attn_kernel
    return pl.pallas_call(
        body,
        grid=(b, n_t),
        in_specs=in_specs,
        out_specs=rows(d),
        out_shape=jax.ShapeDtypeStruct((b, t, d), F32),
        scratch_shapes=[pltpu.VMEM((tt, d), BF16)],
        compiler_params=_params(2),
        name="mem_attn" if mix is None else "mix_mem_attn",
    )(*args)


def _ffn_kernel(x_ref, gpre_ref, wg_ref, wu_ref, wd_ref, gpost_ref, o_ref, acc_ref):
    x = x_ref[...]
    h = _rms(x, gpre_ref[...]).astype(BF16)
    for c in range(D_FF // FF_CHUNK):
        sl = slice(c * FF_CHUNK, (c + 1) * FF_CHUNK)
        gate = _dot(h, wg_ref[:, sl])
        up = _dot(h, wu_ref[:, sl])
        a = (gate * jax.nn.sigmoid(gate) * up).astype(BF16)
        part = _dot(a, wd_ref[sl, :])
        if c == 0:
            acc_ref[...] = part
        else:
            acc_ref[...] += part
    o_ref[...] = x + _rms(acc_ref[...], gpost_ref[...])


def _ffn(x, g_pre, w_gate, w_up, w_down, g_post, *, tt):
    n, d = x.shape
    rows = pl.BlockSpec((tt, d), lambda i: (i, 0))
    return pl.pallas_call(
        _ffn_kernel,
        grid=(n // tt,),
        in_specs=[rows, _const_spec((1, d)), _const_spec((d, D_FF)), _const_spec((d, D_FF)),
                  _const_spec((D_FF, d)), _const_spec((1, d))],
        out_specs=rows,
        out_shape=jax.ShapeDtypeStruct((n, d), F32),
        scratch_shapes=[pltpu.VMEM((tt, d), F32)],
        compiler_params=_params(1),
        name="ffn",
    )(x, g_pre, w_gate, w_up, w_down, g_post)


def kernel(x_prompt, x_sample, cache_att_k, cache_att_v, cache_conv, cache_mem_k, cache_mem_v, mem_prompt, g_mix_pre, g_mix_post, w_in, rel_bias, conv_w, conv_b, cln_g, cln_b, w_out, g_mem_pre, g_mem_post, g_mem_kv, w_mq, w_mk, w_mv, w_mo, g_ffn_pre, g_ffn_post, w_gate, w_up, w_down):
    depth = w_in.shape[0]
    b, t_p, d = x_prompt.shape
    b_s, t_s, _ = x_sample.shape
    past = cache_att_k.shape[2]
    assert d == D_MODEL and t_p % ATT_PAST == 0 and past == ATT_PAST
    assert t_s <= CHUNK and t_s % 16 == 0 and CONV_WIDTH - 1 <= HALO

    row = lambda a: a[None, :]
    xp, xs = x_prompt, x_sample
    outs = [[] for _ in range(8)]
    for l in range(depth):
        bf = lambda w: w[l].astype(BF16)
        w_in_l, w_out_l = bf(w_in), bf(w_out)
        w_mq_l, w_mk_l, w_mv_l, w_mo_l = bf(w_mq), bf(w_mk), bf(w_mv), bf(w_mo)
        w_gate_l, w_up_l, w_down_l = bf(w_gate), bf(w_up), bf(w_down)
        conv = (conv_w[l], row(conv_b[l]), row(cln_g[l]), row(cln_b[l]))

        mk_p, mv_p, mk_pb, mv_pb = _mem_kv(mem_prompt, row(g_mem_kv[l]), w_mk_l, w_mv_l)
        qt, k, vt, c_act, k_last, v_last, u_last = _mixer_in_prompt(
            xp, row(g_mix_pre[l]), w_in_l, *conv)
        att = _band_attn_prompt(qt, k, vt, _band_bias_prompt(rel_bias[l]))
        xp = _mem_attn(xp, mk_pb, mv_pb, row(g_mem_pre[l]), w_mq_l, w_mo_l, row(g_mem_post[l]),
                       tt=ATT_PAST, mix=(att, c_act, w_out_l, row(g_mix_post[l])))
        xp = _ffn(xp.reshape(b * t_p, d), row(g_ffn_pre[l]), w_gate_l, w_up_l, w_down_l,
                  row(g_ffn_post[l]), tt=ATT_PAST).reshape(b, t_p, d)
        outs[0].append(k_last.reshape(b, ATT_PAST, ATT_HEADS, ATT_HEAD_DIM))
        outs[1].append(v_last.reshape(b, ATT_PAST, ATT_HEADS, ATT_HEAD_DIM))
        outs[2].append(u_last[:, HALO - (CONV_WIDTH - 1):])
        outs[3].append(mk_p.reshape(b, N_MEM, MEM_HEADS, MEM_HEAD_DIM))
        outs[4].append(mv_p.reshape(b, N_MEM, MEM_HEADS, MEM_HEAD_DIM))

        q, k_new, v_new, u = _mixer_in_sample(xs.reshape(b_s * t_s, d), row(g_mix_pre[l]), w_in_l)
        u = u.reshape(b_s, t_s, D_CONV)
        att, nk, nv = _band_attn_sample(
            q.reshape(b_s, t_s, D_ATT), k_new.reshape(b_s, t_s, D_ATT),
            v_new.reshape(b_s, t_s, D_ATT), cache_att_k[l].reshape(b_s, past, D_ATT),
            cache_att_v[l].reshape(b_s, past, D_ATT), rel_bias[l])
        halo = jnp.pad(cache_conv[l], ((0, 0), (HALO - (CONV_WIDTH - 1), 0), (0, 0)))
        xs = _mix_out_sample(att, u, halo, xs, *conv, w_out_l, row(g_mix_post[l]))
        xs = _mem_attn(xs, cache_mem_k[l].reshape(b_s, N_MEM, d).astype(BF16),
                       cache_mem_v[l].reshape(b_s, N_MEM, d).astype(BF16),
                       row(g_mem_pre[l]), w_mq_l, w_mo_l, row(g_mem_post[l]), tt=t_s)
        xs = _ffn(xs.reshape(b_s * t_s, d), row(g_ffn_pre[l]), w_gate_l, w_up_l, w_down_l,
                  row(g_ffn_post[l]), tt=b_s * t_s).reshape(b_s, t_s, d)
        outs[5].append(nk.reshape(b_s, past, ATT_HEADS, ATT_HEAD_DIM))
        outs[6].append(nv.reshape(b_s, past, ATT_HEADS, ATT_HEAD_DIM))
        outs[7].append(jnp.concatenate([cache_conv[l], u], axis=1)[:, -(CONV_WIDTH - 1):])

    return (xp, xs) + tuple(jnp.stack(o, 0) for o in outs)
```

```python
import functools

import jax
import jax.numpy as jnp
import numpy as np
from jax import lax
from jax.experimental import pallas as pl
from jax.experimental.pallas import tpu as pltpu

D_MODEL = 1024
CHUNK = 64
ATT_PAST = 512
BAND = ATT_PAST + CHUNK
D_ATT = 512
ATT_HEADS = 8
ATT_HEAD_DIM = 64
D_CONV = 512
CONV_WIDTH = 31
REL_CLIP = 128
N_MEM = 256
MEM_HEADS = 4
MEM_HEAD_DIM = 256
D_FF = 2816
EPS = 1e-6
NEG_INF = -1e30

LANES = 128
SUBLANES = 8
MXU_COLS = 256
GROUP = 4 * CHUNK
N_KEY_TILES = (ATT_PAST + GROUP) // MXU_COLS
ATT_STEP = ATT_PAST
LOG2E = 1.4426950408889634
Q_SCALE = ATT_HEAD_DIM ** -0.5 * LOG2E
KEY_WINDOW = BAND + CHUNK
HALO = 32
FF_CHUNK = MXU_COLS
VMEM_LIMIT = 56 * 1024 * 1024

F32 = jnp.float32
BF16 = jnp.bfloat16


def _params(n_grid):
    return pltpu.CompilerParams(dimension_semantics=("arbitrary",) * n_grid,
                                vmem_limit_bytes=VMEM_LIMIT)


def _const_spec(shape):
    zeros = (0,) * len(shape)
    return pl.BlockSpec(shape, lambda *_: zeros, pipeline_mode=pl.Buffered(1))


def _dot(a, b):
    return jnp.dot(a, b, preferred_element_type=F32)


def _dot_nt(a, b):
    return lax.dot_general(a, b, (((1,), (1,)), ((), ())), preferred_element_type=F32)


def _rms(x, g):
    y = x * lax.rsqrt(jnp.mean(x * x, axis=-1, keepdims=True) + EPS)
    return y * g


def _mem_kv_kernel(m_ref, g_ref, wk_ref, wv_ref, k_ref, v_ref, kb_ref, vb_ref):
    m = _rms(m_ref[...], g_ref[...]).astype(BF16)
    k = _dot(m, wk_ref[...])
    v = _dot(m, wv_ref[...])
    k_ref[...] = k
    v_ref[...] = v
    kb_ref[...] = k.astype(BF16)
    vb_ref[...] = v.astype(BF16)


def _mem_kv(mem, g, wk, wv):
    b, n, d = mem.shape
    row = pl.BlockSpec((None, n, d), lambda i: (i, 0, 0))
    return pl.pallas_call(
        _mem_kv_kernel,
        grid=(b,),
        in_specs=[row, _const_spec((1, d)), _const_spec((d, d)), _const_spec((d, d))],
        out_specs=[row, row, row, row],
        out_shape=[jax.ShapeDtypeStruct((b, n, d), F32)] * 2
        + [jax.ShapeDtypeStruct((b, n, d), BF16)] * 2,
        compiler_params=_params(1),
        name="mem_kv",
    )(mem, g, wk, wv)


def _mixer_proj(x_ref, g_ref, w_ref):
    h = _rms(x_ref[...], g_ref[...]).astype(BF16)
    edges = (0, D_ATT, 2 * D_ATT, 3 * D_ATT, 3 * D_ATT + D_CONV, 3 * D_ATT + 2 * D_CONV)
    raw = [_dot(h, w_ref[:, a:b]) for a, b in zip(edges[:-1], edges[1:])]
    q, k, v, u_val, u_gate = raw
    return q * Q_SCALE, k, v, u_val * jax.nn.sigmoid(u_gate), raw


def _zero_after(x):
    bits = lax.bitcast_convert_type(x, jnp.uint32)
    zero = lax.shift_right_logical(lax.shift_right_logical(bits, jnp.uint32(16)), jnp.uint32(16))
    return lax.bitcast_convert_type(zero, F32)


def _conv_module(window, cw_ref, cb_ref, lg_ref, lb_ref, conv_buf, out_ref, tt, gate=None):
    lead = HALO - (CONV_WIDTH - 1)
    rs = min(tt, CHUNK)
    blocks = rs // SUBLANES
    strips = D_CONV // LANES
    for r0 in range(0, tt, rs):
        for c in range(strips):
            sl = slice(c * LANES, (c + 1) * LANES)
            taps = [jnp.broadcast_to(cw_ref[w:w + 1, sl], (SUBLANES, LANES))
                    for w in range(CONV_WIDTH)]
            bias = jnp.broadcast_to(cb_ref[:, sl], (SUBLANES, LANES))
            if gate is not None:
                bias = bias + gate((r0 // rs) * strips + c)
            accs = [bias] * blocks
            for d in range(SUBLANES * (blocks - 1) + CONV_WIDTH):
                win = window(c, r0 + lead + d)
                for b in range(blocks):
                    w = d - SUBLANES * b
                    if 0 <= w < CONV_WIDTH:
                        accs[b] = accs[b] + win * taps[w]
            for b in range(blocks):
                conv_buf[r0 + b * SUBLANES:r0 + (b + 1) * SUBLANES, sl] = accs[b]

        acc = conv_buf[r0:r0 + rs, :]
        mu = jnp.mean(acc, axis=-1, keepdims=True)
        xc = acc - mu
        var = jnp.mean(xc * xc, axis=-1, keepdims=True)
        y = xc * lax.rsqrt(var + EPS) * lg_ref[...] + lb_ref[...]
        out_ref[r0:r0 + rs, :] = (y * jax.nn.sigmoid(y)).astype(BF16)


def _mixer_in_prompt_kernel(x_ref, g_ref, w_ref, cw_ref, cb_ref, lg_ref, lb_ref,
                            qt_ref, k_ref, vt_ref, c_ref, kl_ref, vl_ref, ul_ref,
                            u_buf, u_next, conv_buf, *, tt):
    strips = D_CONV // LANES

    @pl.when(pl.program_id(1) == 0)
    def _():
        u_buf[...] = jnp.zeros(u_buf.shape, F32)

    q, k, v, u, raw = _mixer_proj(x_ref, g_ref, w_ref)
    qt_ref[...] = q.T.astype(BF16)
    k_ref[...] = k.astype(BF16)
    vt_ref[...] = v.T.astype(BF16)
    kl_ref[...] = k
    vl_ref[...] = v
    ul_ref[...] = u[tt - HALO:tt, :]
    u_next[...] = u

    n_units = (tt // CHUNK) * strips
    col_tiles = D_ATT // MXU_COLS
    n_pass = len(raw) * col_tiles

    def gate(n):
        pos = (n + 1) * n_pass / (n_units + 1)
        dot_i, nt = divmod(int(pos), col_tiles)
        r = int((pos - int(pos)) * (tt // SUBLANES)) * SUBLANES
        return _zero_after(raw[dot_i][r:r + SUBLANES, nt * MXU_COLS:nt * MXU_COLS + LANES])

    window = lambda c, start: u_buf[c, pl.ds(start, SUBLANES, stride=1), :]
    _conv_module(window, cw_ref, cb_ref, lg_ref, lb_ref, conv_buf, c_ref, tt, gate=gate)

    for c in range(strips):
        u_buf[c, 0:HALO, :] = u_buf[c, tt:tt + HALO, :]
        u_buf[c, HALO:HALO + tt, :] = u_next[:, c * LANES:(c + 1) * LANES]


def _mixer_in_prompt(x, g, w_in, conv_w, conv_b, cln_g, cln_b):
    b, t, d = x.shape
    tt = ATT_PAST
    n_t = t // tt
    d_in = w_in.shape[1]
    cur = lambda j: jnp.minimum(j, n_t - 1)
    lag = lambda j: jnp.maximum(j - 1, 0)
    rows = lambda c, f: pl.BlockSpec((None, tt, c), lambda i, j: (i, f(j), 0))
    cols = pl.BlockSpec((None, None, D_ATT, tt), lambda i, j: (i, cur(j), 0, 0))
    last = lambda r: pl.BlockSpec((None, r, D_ATT), lambda i, j: (i, 0, 0))
    return pl.pallas_call(
        functools.partial(_mixer_in_prompt_kernel, tt=tt),
        grid=(b, n_t + 1),
        in_specs=[rows(d, cur), _const_spec((1, d)), _const_spec((d, d_in)),
                  _const_spec(conv_w.shape), _const_spec((1, D_CONV)), _const_spec((1, D_CONV)),
                  _const_spec((1, D_CONV))],
        out_specs=[cols, rows(D_ATT, cur), cols, rows(D_CONV, lag), last(tt), last(tt),
                   last(HALO)],
        out_shape=[jax.ShapeDtypeStruct((b, n_t, D_ATT, tt), BF16),
                   jax.ShapeDtypeStruct((b, t, D_ATT), BF16),
                   jax.ShapeDtypeStruct((b, n_t, D_ATT, tt), BF16),
                   jax.ShapeDtypeStruct((b, t, D_CONV), BF16),
                   jax.ShapeDtypeStruct((b, tt, D_ATT), F32),
                   jax.ShapeDtypeStruct((b, tt, D_ATT), F32),
                   jax.ShapeDtypeStruct((b, HALO, D_CONV), F32)],
        scratch_shapes=[pltpu.VMEM((D_CONV // LANES, HALO + tt, LANES), F32),
                        pltpu.VMEM((tt, D_CONV), F32), pltpu.VMEM((tt, D_CONV), F32)],
        compiler_params=_params(2),
        name="mixer_in_prompt",
    )(x, g, w_in, conv_w, conv_b, cln_g, cln_b)


def _mixer_in_sample_kernel(x_ref, g_ref, w_ref, q_ref, k_ref, v_ref, u_ref):
    q, k, v, u, _ = _mixer_proj(x_ref, g_ref, w_ref)
    q_ref[...] = q.astype(BF16)
    k_ref[...] = k
    v_ref[...] = v
    u_ref[...] = u


def _mixer_in_sample(x, g, w_in):
    n, d = x.shape
    d_in = w_in.shape[1]
    full = lambda c: pl.BlockSpec((n, c), lambda i: (0, 0))
    return pl.pallas_call(
        _mixer_in_sample_kernel,
        grid=(1,),
        in_specs=[full(d), _const_spec((1, d)), _const_spec((d, d_in))],
        out_specs=[full(D_ATT), full(D_ATT), full(D_ATT), full(D_CONV)],
        out_shape=[jax.ShapeDtypeStruct((n, D_ATT), BF16),
                   jax.ShapeDtypeStruct((n, D_ATT), F32),
                   jax.ShapeDtypeStruct((n, D_ATT), F32),
                   jax.ShapeDtypeStruct((n, D_CONV), F32)],
        compiler_params=_params(1),
        name="mixer_in_sample",
    )(x, g, w_in)


def _stack_head_pair(q2):
    lane = lax.broadcasted_iota(jnp.int32, q2.shape, 1)
    zero = jnp.zeros_like(q2)
    return jnp.concatenate([jnp.where(lane < ATT_HEAD_DIM, q2, zero),
                            jnp.where(lane >= ATT_HEAD_DIM, q2, zero)], axis=0)


def _merge_head_pair(o):
    m = o.shape[0] // 2
    lane = lax.broadcasted_iota(jnp.int32, (m, LANES), 1)
    return jnp.where(lane < ATT_HEAD_DIM, o[:m], o[m:])


def _band_attn_kernel(qt_ref, kp_ref, kc_ref, vtp_ref, vtc_ref, bias_ref, o_ref, s_buf, p_buf):
    first = pl.program_id(1) == 0
    row = lax.broadcasted_iota(jnp.int32, (LANES, GROUP), 0)
    items = [(grp, p) for grp in range(ATT_STEP // GROUP) for p in range(ATT_HEADS // 2)]
    tiles_per_block = ATT_STEP // MXU_COLS
    n_lt = 2 * GROUP // LANES

    def tile_slice(tile):
        t = tile % tiles_per_block
        return slice(t * MXU_COLS, (t + 1) * MXU_COLS)

    def scores(n):
        grp, p = items[n]
        rows = slice(p * LANES, (p + 1) * LANES)
        qt2 = qt_ref[rows, grp * GROUP:(grp + 1) * GROUP]
        zero = jnp.zeros_like(qt2)
        qm = jnp.concatenate([jnp.where(row < ATT_HEAD_DIM, qt2, zero),
                              jnp.where(row >= ATT_HEAD_DIM, qt2, zero)], axis=1)
        for j in range(N_KEY_TILES):
            tile = grp + j
            k_ref = kp_ref if tile < tiles_per_block else kc_ref
            s = _dot(k_ref[tile_slice(tile), rows], qm)
            for lt in range(n_lt):
                s_buf[n % 2, lt, j * MXU_COLS:(j + 1) * MXU_COLS, :] = s[:, lt * LANES:(lt + 1) * LANES]

    def finish(n):
        grp, p = items[n]
        slot = n % 2
        rows = slice(p * LANES, (p + 1) * LANES)
        variant = jnp.where(first, 1 + grp, 0)
        inv_l = []
        for lt in range(n_lt):
            hh, cp = divmod(lt, GROUP // LANES)
            w0 = LANES * cp
            t = s_buf[slot, lt, w0:w0 + KEY_WINDOW, :] + bias_ref[variant, 2 * p + hh, cp]
            m = jnp.max(t, axis=0, keepdims=True)
            e = jnp.exp2(t - m)
            inv_l.append(1.0 / jnp.sum(e, axis=0, keepdims=True))
            p_buf[slot, lt, w0:w0 + KEY_WINDOW, :] = e.astype(BF16)
            z0 = KEY_WINDOW if cp == 0 else 0
            p_buf[slot, lt, z0:z0 + LANES, :] = jnp.zeros((LANES, LANES), BF16)
        o = None
        for j in range(N_KEY_TILES):
            tile = grp + j
            vt_ref = vtp_ref if tile < tiles_per_block else vtc_ref
            pt = jnp.concatenate([p_buf[slot, lt, j * MXU_COLS:(j + 1) * MXU_COLS, :]
                                  for lt in range(n_lt)], axis=1)
            d = _dot(vt_ref[rows, tile_slice(tile)], pt)
            o = d if o is None else o + d
        o = o * jnp.concatenate(inv_l, axis=1)
        ot = jnp.where(row < ATT_HEAD_DIM, o[:, 0:GROUP], o[:, GROUP:2 * GROUP])
        o_ref[grp * GROUP:(grp + 1) * GROUP, rows] = ot.T.astype(BF16)

    scores(0)
    for n in range(len(items)):
        if n + 1 < len(items):
            scores(n + 1)
        finish(n)


def _toeplitz_bias(rel_bias, n_rows, n_cols, shift):
    n_heads = rel_bias.shape[0]
    period = n_cols + n_rows
    e0 = shift - REL_CLIP
    pad = period + abs(e0)
    table = jnp.pad(rel_bias[:, ::-1].astype(F32), ((0, 0), (pad, pad)), mode="edge")
    base = pad - e0
    vec = jnp.concatenate([table[:, base:base + n_cols + 1],
                           table[:, base - (n_rows - 1):base]], axis=1)
    flat = jnp.tile(vec, (1, n_rows))[:, :n_rows * (period - 1)]
    return flat.reshape(n_heads, n_rows, period - 1)[:, :, :n_cols]


def _band_bias_prompt(rel_bias):
    base = _toeplitz_bias(rel_bias, CHUNK, KEY_WINDOW + CHUNK, ATT_PAST + CHUNK) * LOG2E
    base = jnp.swapaxes(base, 1, 2)
    window = jnp.concatenate([base[:, CHUNK:CHUNK + KEY_WINDOW], base[:, 0:KEY_WINDOW]], axis=2)

    n_cp = GROUP // LANES
    cp = np.arange(n_cp)[None, :, None, None]
    kk = LANES * cp + np.arange(KEY_WINDOW)[None, None, :, None]
    kj = kk - CHUNK * (2 * cp + np.arange(LANES)[None, None, None, :] // CHUNK)
    first_valid = np.array([0, ATT_PAST, ATT_PAST - GROUP])[:, None, None, None]
    valid = (kj >= 0) & (kj < BAND) & (kk >= first_valid)
    return jnp.where(valid[:, None], window[None, :, None], NEG_INF)


def _band_attn_prompt(qt, k, vt, bias):
    b, t, _ = k.shape
    prev = lambda g: jnp.maximum(g - 1, 0)
    cols = lambda f: pl.BlockSpec((None, None, D_ATT, ATT_STEP), lambda i, g: (i, f(g), 0, 0))
    rows = lambda f: pl.BlockSpec((None, ATT_STEP, D_ATT), lambda i, g: (i, f(g), 0))
    same = lambda g: g
    return pl.pallas_call(
        _band_attn_kernel,
        grid=(b, t // ATT_STEP),
        in_specs=[cols(same), rows(prev), rows(same), cols(prev), cols(same),
                  _const_spec(bias.shape)],
        out_specs=rows(same),
        out_shape=jax.ShapeDtypeStruct((b, t, D_ATT), BF16),
        scratch_shapes=[pltpu.VMEM((2, 2 * GROUP // LANES, N_KEY_TILES * MXU_COLS, LANES), F32),
                        pltpu.VMEM((2, 2 * GROUP // LANES, N_KEY_TILES * MXU_COLS, LANES), BF16)],
        compiler_params=_params(2),
        name="band_attn_prompt",
    )(qt, k, k, vt, vt, bias)


def _band_attn_sample_kernel(q_ref, kn_ref, vn_ref, ck_ref, cv_ref, bias_ref,
                             o_ref, nk_ref, nv_ref, k_buf, v_buf, *, t_s):
    past = ck_ref.shape[0]
    n_keys = k_buf.shape[0]
    nk_ref[0:past - t_s, :] = ck_ref[t_s:past, :]
    nk_ref[past - t_s:past, :] = kn_ref[...]
    nv_ref[0:past - t_s, :] = cv_ref[t_s:past, :]
    nv_ref[past - t_s:past, :] = vn_ref[...]

    k_buf[0:past, :] = ck_ref[...].astype(BF16)
    k_buf[past:past + t_s, :] = kn_ref[...].astype(BF16)
    k_buf[past + t_s:n_keys, :] = jnp.zeros((n_keys - past - t_s, D_ATT), BF16)
    v_buf[0:past, :] = cv_ref[...].astype(BF16)
    v_buf[past:past + t_s, :] = vn_ref[...].astype(BF16)
    v_buf[past + t_s:n_keys, :] = jnp.zeros((n_keys - past - t_s, D_ATT), BF16)

    for p in range(ATT_HEADS // 2):
        sl = slice(p * LANES, (p + 1) * LANES)
        qm = _stack_head_pair(q_ref[:, sl])
        bias = jnp.concatenate([bias_ref[2 * p], bias_ref[2 * p + 1]], axis=0)
        s = _dot_nt(qm, k_buf[:, sl]) + bias
        m = jnp.max(s, axis=-1, keepdims=True)
        e = jnp.exp2(s - m)
        inv_l = 1.0 / jnp.sum(e, axis=-1, keepdims=True)
        o = _dot(e.astype(BF16), v_buf[:, sl]) * inv_l
        o_ref[:, sl] = _merge_head_pair(o).astype(BF16)


def _band_bias_sample(rel_bias, t_s, n_keys):
    bias = _toeplitz_bias(rel_bias, t_s, n_keys, ATT_PAST) * LOG2E
    return jnp.where(jnp.arange(n_keys) < ATT_PAST + t_s, bias, NEG_INF)


def _band_attn_sample(q, k_new, v_new, cache_k, cache_v, rel_bias):
    b, t_s, _ = q.shape
    past = cache_k.shape[1]
    n_keys = past + LANES
    bias = _band_bias_sample(rel_bias, t_s, n_keys)
    new = pl.BlockSpec((None, t_s, D_ATT), lambda i: (i, 0, 0))
    buf = pl.BlockSpec((None, past, D_ATT), lambda i: (i, 0, 0))
    return pl.pallas_call(
        functools.partial(_band_attn_sample_kernel, t_s=t_s),
        grid=(b,),
        in_specs=[new, new, new, buf, buf, _const_spec(bias.shape)],
        out_specs=[new, buf, buf],
        out_shape=[jax.ShapeDtypeStruct((b, t_s, D_ATT), BF16),
                   jax.ShapeDtypeStruct((b, past, D_ATT), F32),
                   jax.ShapeDtypeStruct((b, past, D_ATT), F32)],
        scratch_shapes=[pltpu.VMEM((n_keys, D_ATT), BF16), pltpu.VMEM((n_keys, D_ATT), BF16)],
        compiler_params=_params(1),
        name="band_attn_sample",
    )(q, k_new, v_new, cache_k, cache_v, bias)


def _out_proj_residual(att, c, x, wo_ref, gp_ref):
    y = _dot(att, wo_ref[0:D_ATT, :]) + _dot(c, wo_ref[D_ATT:D_ATT + D_CONV, :])
    return x + _rms(y, gp_ref[...])


def _mem_attn_heads(q, streams, a_buf):
    blocks = [(r0, n, mk, mv, slice(hh * MEM_HEAD_DIM, (hh + 1) * MEM_HEAD_DIM))
              for r0, n, mk, mv in streams for hh in range(MEM_HEADS)]
    scores = [_dot_nt(q[r0:r0 + n, sl], mk[:, sl]) for r0, n, mk, _, sl in blocks]
    probs = []
    for s in scores:
        e = jnp.exp(s - jnp.max(s, axis=-1, keepdims=True))
        probs.append((e.astype(BF16), 1.0 / jnp.sum(e, axis=-1, keepdims=True)))
    for (r0, n, _, mv, sl), (e, inv_l) in zip(blocks, probs):
        a_buf[r0:r0 + n, sl] = (_dot(e, mv[:, sl]) * inv_l).astype(BF16)


def _mem_query(x, gpre_ref, wq_ref):
    h = _rms(x, gpre_ref[...]).astype(BF16)
    return (_dot(h, wq_ref[...]) * (MEM_HEAD_DIM ** -0.5)).astype(BF16)


def _sample_tail_kernel(att_ref, u_ref, halo_ref, x_ref, cw_ref, cb_ref, lg_ref, lb_ref,
                        wout_ref, gmix_ref, mk_ref, mv_ref, gpre_ref, wq_ref, wo_ref, gpost_ref,
                        o_ref, u_buf, conv_buf, c_buf, a_buf, *, t_s):
    n_streams = u_ref.shape[0]
    for b in range(n_streams):
        for c in range(D_CONV // LANES):
            sl = slice(c * LANES, (c + 1) * LANES)
            u_buf[c, b, 0:HALO, :] = halo_ref[b, :, sl]
            u_buf[c, b, HALO:HALO + t_s, :] = u_ref[b, :, sl]
        window = lambda c, start, b=b: u_buf[c, b, pl.ds(start, SUBLANES, stride=1), :]
        rows = pl.ds(b * t_s, t_s)
        _conv_module(window, cw_ref, cb_ref, lg_ref, lb_ref, conv_buf.at[rows], c_buf.at[rows], t_s)
    x = _out_proj_residual(att_ref[...], c_buf[...], x_ref[...], wout_ref, gmix_ref)
    q = _mem_query(x, gpre_ref, wq_ref)
    _mem_attn_heads(q, [(b * t_s, t_s, mk_ref.at[b], mv_ref.at[b]) for b in range(n_streams)], a_buf)
    o_ref[...] = x + _rms(_dot(a_buf[...], wo_ref[...]), gpost_ref[...])


def _sample_tail(att, u, halo, x, conv_w, conv_b, cln_g, cln_b, w_out, g_mix_post,
                 mk, mv, g_pre, w_mq, w_mo, g_post):
    b_s, t_s, _ = u.shape
    n, d = x.shape
    operands = (att, u, halo, x, conv_w, conv_b, cln_g, cln_b, w_out, g_mix_post,
                mk, mv, g_pre, w_mq, w_mo, g_post)
    return pl.pallas_call(
        functools.partial(_sample_tail_kernel, t_s=t_s),
        grid=(1,),
        in_specs=[_const_spec(a.shape) for a in operands],
        out_specs=pl.BlockSpec((n, d), lambda i: (0, 0)),
        out_shape=jax.ShapeDtypeStruct((n, d), F32),
        scratch_shapes=[pltpu.VMEM((D_CONV // LANES, b_s, HALO + t_s, LANES), F32),
                        pltpu.VMEM((n, D_CONV), F32), pltpu.VMEM((n, D_CONV), BF16),
                        pltpu.VMEM((n, d), BF16)],
        compiler_params=_params(1),
        name="sample_tail",
    )(*operands)


def _mix_mem_attn_kernel(att_ref, c_ref, x_ref, wout_ref, gmix_ref, mk_ref, mv_ref, gpre_ref,
                         wq_ref, wo_ref, gpost_ref, o_ref, a_buf):
    x = _out_proj_residual(att_ref[...], c_ref[...], x_ref[...], wout_ref, gmix_ref)
    q = _mem_query(x, gpre_ref, wq_ref)
    _mem_attn_heads(q, [(0, x.shape[0], mk_ref, mv_ref)], a_buf)
    o_ref[...] = x + _rms(_dot(a_buf[...], wo_ref[...]), gpost_ref[...])


def _mix_mem_attn(att, c, x, w_out, g_mix_post, mk, mv, g_pre, w_mq, w_mo, g_post, *, tt):
    b, t, d = x.shape
    rows = lambda n: pl.BlockSpec((None, tt, n), lambda i, j: (i, j, 0))
    mem = pl.BlockSpec((None, N_MEM, d), lambda i, j: (i, 0, 0))
    return pl.pallas_call(
        _mix_mem_attn_kernel,
        grid=(b, t // tt),
        in_specs=[rows(D_ATT), rows(D_CONV), rows(d), _const_spec(w_out.shape), _const_spec((1, d)),
                  mem, mem, _const_spec((1, d)), _const_spec((d, d)), _const_spec((d, d)),
                  _const_spec((1, d))],
        out_specs=rows(d),
        out_shape=jax.ShapeDtypeStruct((b, t, d), F32),
        scratch_shapes=[pltpu.VMEM((tt, d), BF16)],
        compiler_params=_params(2),
        name="mix_mem_attn",
    )(att, c, x, w_out, g_mix_post, mk, mv, g_pre, w_mq, w_mo, g_post)


def _ffn_kernel(x_ref, gpre_ref, wg_ref, wu_ref, wd_ref, gpost_ref, o_ref, acc_ref):
    x = x_ref[...]
    h = _rms(x, gpre_ref[...]).astype(BF16)
    for c in range(D_FF // FF_CHUNK):
        sl = slice(c * FF_CHUNK, (c + 1) * FF_CHUNK)
        gate = _dot(h, wg_ref[:, sl])
        up = _dot(h, wu_ref[:, sl])
        a = (gate * jax.nn.sigmoid(gate) * up).astype(BF16)
        part = _dot(a, wd_ref[sl, :])
        if c == 0:
            acc_ref[...] = part
        else:
            acc_ref[...] += part
    o_ref[...] = x + _rms(acc_ref[...], gpost_ref[...])


def _ffn(x, g_pre, w_gate, w_up, w_down, g_post, *, tt):
    n, d = x.shape
    rows = pl.BlockSpec((tt, d), lambda i: (i, 0))
    return pl.pallas_call(
        _ffn_kernel,
        grid=(n // tt,),
        in_specs=[rows, _const_spec((1, d)), _const_spec((d, D_FF)), _const_spec((d, D_FF)),
                  _const_spec((D_FF, d)), _const_spec((1, d))],
        out_specs=rows,
        out_shape=jax.ShapeDtypeStruct((n, d), F32),
        scratch_shapes=[pltpu.VMEM((tt, d), F32)],
        compiler_params=_params(1),
        name="ffn",
    )(x, g_pre, w_gate, w_up, w_down, g_post)


def kernel(x_prompt, x_sample, cache_att_k, cache_att_v, cache_conv, cache_mem_k, cache_mem_v, mem_prompt, g_mix_pre, g_mix_post, w_in, rel_bias, conv_w, conv_b, cln_g, cln_b, w_out, g_mem_pre, g_mem_post, g_mem_kv, w_mq, w_mk, w_mv, w_mo, g_ffn_pre, g_ffn_post, w_gate, w_up, w_down):
    depth = w_in.shape[0]
    b, t_p, d = x_prompt.shape
    b_s, t_s, _ = x_sample.shape
    past = cache_att_k.shape[2]
    assert d == D_MODEL and t_p % ATT_PAST == 0 and past == ATT_PAST
    assert t_s <= CHUNK and t_s % 16 == 0 and CONV_WIDTH - 1 <= HALO

    row = lambda a: a[None, :]
    xp, xs = x_prompt, x_sample
    outs = [[] for _ in range(8)]
    for l in range(depth):
        bf = lambda w: w[l].astype(BF16)
        w_in_l, w_out_l = bf(w_in), bf(w_out)
        w_mq_l, w_mk_l, w_mv_l, w_mo_l = bf(w_mq), bf(w_mk), bf(w_mv), bf(w_mo)
        w_gate_l, w_up_l, w_down_l = bf(w_gate), bf(w_up), bf(w_down)
        conv = (conv_w[l], row(conv_b[l]), row(cln_g[l]), row(cln_b[l]))

        mk_p, mv_p, mk_pb, mv_pb = _mem_kv(mem_prompt, row(g_mem_kv[l]), w_mk_l, w_mv_l)
        qt, k, vt, c_act, k_last, v_last, u_last = _mixer_in_prompt(
            xp, row(g_mix_pre[l]), w_in_l, *conv)
        att = _band_attn_prompt(qt, k, vt, _band_bias_prompt(rel_bias[l]))
        mem_w = (row(g_mem_pre[l]), w_mq_l, w_mo_l, row(g_mem_post[l]))
        xp = _mix_mem_attn(att, c_act, xp, w_out_l, row(g_mix_post[l]), mk_pb, mv_pb, *mem_w,
                           tt=ATT_PAST)
        xp = _ffn(xp.reshape(b * t_p, d), row(g_ffn_pre[l]), w_gate_l, w_up_l, w_down_l,
                  row(g_ffn_post[l]), tt=ATT_PAST).reshape(b, t_p, d)
        outs[0].append(k_last.reshape(b, ATT_PAST, ATT_HEADS, ATT_HEAD_DIM))
        outs[1].append(v_last.reshape(b, ATT_PAST, ATT_HEADS, ATT_HEAD_DIM))
        outs[2].append(u_last[:, HALO - (CONV_WIDTH - 1):])
        outs[3].append(mk_p.reshape(b, N_MEM, MEM_HEADS, MEM_HEAD_DIM))
        outs[4].append(mv_p.reshape(b, N_MEM, MEM_HEADS, MEM_HEAD_DIM))

        q, k_new, v_new, u = _mixer_in_sample(xs.reshape(b_s * t_s, d), row(g_mix_pre[l]), w_in_l)
        u = u.reshape(b_s, t_s, D_CONV)
        att, nk, nv = _band_attn_sample(
            q.reshape(b_s, t_s, D_ATT), k_new.reshape(b_s, t_s, D_ATT),
            v_new.reshape(b_s, t_s, D_ATT), cache_att_k[l].reshape(b_s, past, D_ATT),
            cache_att_v[l].reshape(b_s, past, D_ATT), rel_bias[l])
        halo = jnp.pad(cache_conv[l], ((0, 0), (HALO - (CONV_WIDTH - 1), 0), (0, 0)))
        xs = _sample_tail(att.reshape(b_s * t_s, D_ATT), u, halo, xs.reshape(b_s * t_s, d), *conv,
                          w_out_l, row(g_mix_post[l]),
                          cache_mem_k[l].reshape(b_s, N_MEM, d).astype(BF16),
                          cache_mem_v[l].reshape(b_s, N_MEM, d).astype(BF16), *mem_w)
        xs = _ffn(xs, row(g_ffn_pre[l]), w_gate_l, w_up_l, w_down_l,
                  row(g_ffn_post[l]), tt=b_s * t_s).reshape(b_s, t_s, d)
        outs[5].append(nk.reshape(b_s, past, ATT_HEADS, ATT_HEAD_DIM))
        outs[6].append(nv.reshape(b_s, past, ATT_HEADS, ATT_HEAD_DIM))
        outs[7].append(jnp.concatenate([cache_conv[l], u], axis=1)[:, -(CONV_WIDTH - 1):])

    return (xp, xs) + tuple(jnp.stack(o, 0) for o in outs)
```

```python
import functools

import jax
import jax.numpy as jnp
import numpy as np
from jax import lax
from jax.experimental import pallas as pl
from jax.experimental.pallas import tpu as pltpu

D_MODEL = 1024
CHUNK = 64
ATT_PAST = 512
BAND = ATT_PAST + CHUNK
D_ATT = 512
ATT_HEADS = 8
ATT_HEAD_DIM = 64
D_CONV = 512
CONV_WIDTH = 31
REL_CLIP = 128
N_MEM = 256
MEM_HEADS = 4
MEM_HEAD_DIM = 256
D_FF = 2816
EPS = 1e-6
NEG_INF = -1e30

LANES = 128
SUBLANES = 8
MXU_COLS = 256
GROUP = 4 * CHUNK
N_KEY_TILES = (ATT_PAST + GROUP) // MXU_COLS
ATT_STEP = ATT_PAST
LOG2E = 1.4426950408889634
Q_SCALE = ATT_HEAD_DIM ** -0.5 * LOG2E
KEY_WINDOW = BAND + CHUNK
HALO = 32
FF_CHUNK = MXU_COLS
FFN_ROWS = 1024
VMEM_LIMIT = 56 * 1024 * 1024

F32 = jnp.float32
BF16 = jnp.bfloat16


def _params(n_grid):
    return pltpu.CompilerParams(dimension_semantics=("arbitrary",) * n_grid,
                                vmem_limit_bytes=VMEM_LIMIT)


def _const_spec(shape):
    zeros = (0,) * len(shape)
    return pl.BlockSpec(shape, lambda *_: zeros, pipeline_mode=pl.Buffered(1))


def _dot(a, b):
    return jnp.dot(a, b, preferred_element_type=F32)


def _dot_nt(a, b):
    return lax.dot_general(a, b, (((1,), (1,)), ((), ())), preferred_element_type=F32)


def _rms(x, g):
    y = x * lax.rsqrt(jnp.mean(x * x, axis=-1, keepdims=True) + EPS)
    return y * g


def _mem_kv_kernel(m_ref, g_ref, wk_ref, wv_ref, k_ref, v_ref, kb_ref, vb_ref):
    m = _rms(m_ref[...], g_ref[...]).astype(BF16)
    k = _dot(m, wk_ref[...])
    v = _dot(m, wv_ref[...])
    k_ref[...] = k
    v_ref[...] = v
    kb_ref[...] = k.astype(BF16)
    vb_ref[...] = v.astype(BF16)


def _mem_kv(mem, g, wk, wv):
    b, n, d = mem.shape
    row = pl.BlockSpec((None, n, d), lambda i: (i, 0, 0))
    return pl.pallas_call(
        _mem_kv_kernel,
        grid=(b,),
        in_specs=[row, _const_spec((1, d)), _const_spec((d, d)), _const_spec((d, d))],
        out_specs=[row, row, row, row],
        out_shape=[jax.ShapeDtypeStruct((b, n, d), F32)] * 2
        + [jax.ShapeDtypeStruct((b, n, d), BF16)] * 2,
        compiler_params=_params(1),
        name="mem_kv",
    )(mem, g, wk, wv)


def _mixer_proj(x_ref, g_ref, w_ref):
    h = _rms(x_ref[...], g_ref[...]).astype(BF16)
    edges = (0, D_ATT, 2 * D_ATT, 3 * D_ATT, 3 * D_ATT + D_CONV, 3 * D_ATT + 2 * D_CONV)
    raw = [_dot(h, w_ref[:, a:b]) for a, b in zip(edges[:-1], edges[1:])]
    q, k, v, u_val, u_gate = raw
    return q * Q_SCALE, k, v, u_val * jax.nn.sigmoid(u_gate), raw


def _zero_after(x):
    bits = lax.bitcast_convert_type(x, jnp.uint32)
    zero = lax.shift_right_logical(lax.shift_right_logical(bits, jnp.uint32(16)), jnp.uint32(16))
    return lax.bitcast_convert_type(zero, F32)


def _conv_module(window, cw_ref, cb_ref, lg_ref, lb_ref, conv_buf, out_ref, tt, gate=None):
    lead = HALO - (CONV_WIDTH - 1)
    rs = min(tt, CHUNK)
    blocks = rs // SUBLANES
    strips = D_CONV // LANES
    for r0 in range(0, tt, rs):
        for c in range(strips):
            sl = slice(c * LANES, (c + 1) * LANES)
            taps = [jnp.broadcast_to(cw_ref[w:w + 1, sl], (SUBLANES, LANES))
                    for w in range(CONV_WIDTH)]
            bias = jnp.broadcast_to(cb_ref[:, sl], (SUBLANES, LANES))
            if gate is not None:
                bias = bias + gate((r0 // rs) * strips + c)
            accs = [bias] * blocks
            for d in range(SUBLANES * (blocks - 1) + CONV_WIDTH):
                win = window(c, r0 + lead + d)
                for b in range(blocks):
                    w = d - SUBLANES * b
                    if 0 <= w < CONV_WIDTH:
                        accs[b] = accs[b] + win * taps[w]
            for b in range(blocks):
                conv_buf[r0 + b * SUBLANES:r0 + (b + 1) * SUBLANES, sl] = accs[b]

        acc = conv_buf[r0:r0 + rs, :]
        mu = jnp.mean(acc, axis=-1, keepdims=True)
        xc = acc - mu
        var = jnp.mean(xc * xc, axis=-1, keepdims=True)
        y = xc * lax.rsqrt(var + EPS) * lg_ref[...] + lb_ref[...]
        out_ref[r0:r0 + rs, :] = (y * jax.nn.sigmoid(y)).astype(BF16)


def _mixer_in_prompt_kernel(x_ref, g_ref, w_ref, cw_ref, cb_ref, lg_ref, lb_ref,
                            qt_ref, k_ref, vt_ref, c_ref, kl_ref, vl_ref, ul_ref,
                            u_buf, u_next, conv_buf, *, tt):
    strips = D_CONV // LANES

    @pl.when(pl.program_id(1) == 0)
    def _():
        u_buf[...] = jnp.zeros(u_buf.shape, F32)

    q, k, v, u, raw = _mixer_proj(x_ref, g_ref, w_ref)
    qt_ref[...] = q.T.astype(BF16)
    k_ref[...] = k.astype(BF16)
    vt_ref[...] = v.T.astype(BF16)
    kl_ref[...] = k
    vl_ref[...] = v
    ul_ref[...] = u[tt - HALO:tt, :]
    u_next[...] = u

    n_units = (tt // CHUNK) * strips
    col_tiles = D_ATT // MXU_COLS
    n_pass = len(raw) * col_tiles

    def gate(n):
        pos = (n + 1) * n_pass / (n_units + 1)
        dot_i, nt = divmod(int(pos), col_tiles)
        r = int((pos - int(pos)) * (tt // SUBLANES)) * SUBLANES
        return _zero_after(raw[dot_i][r:r + SUBLANES, nt * MXU_COLS:nt * MXU_COLS + LANES])

    window = lambda c, start: u_buf[c, pl.ds(start, SUBLANES, stride=1), :]
    _conv_module(window, cw_ref, cb_ref, lg_ref, lb_ref, conv_buf, c_ref, tt, gate=gate)

    for c in range(strips):
        u_buf[c, 0:HALO, :] = u_buf[c, tt:tt + HALO, :]
        u_buf[c, HALO:HALO + tt, :] = u_next[:, c * LANES:(c + 1) * LANES]


def _mixer_in_prompt(x, g, w_in, conv_w, conv_b, cln_g, cln_b):
    b, t, d = x.shape
    tt = ATT_PAST
    n_t = t // tt
    d_in = w_in.shape[1]
    cur = lambda j: jnp.minimum(j, n_t - 1)
    lag = lambda j: jnp.maximum(j - 1, 0)
    rows = lambda c, f: pl.BlockSpec((None, tt, c), lambda i, j: (i, f(j), 0))
    cols = pl.BlockSpec((None, None, D_ATT, tt), lambda i, j: (i, cur(j), 0, 0))
    last = lambda r: pl.BlockSpec((None, r, D_ATT), lambda i, j: (i, 0, 0))
    return pl.pallas_call(
        functools.partial(_mixer_in_prompt_kernel, tt=tt),
        grid=(b, n_t + 1),
        in_specs=[rows(d, cur), _const_spec((1, d)), _const_spec((d, d_in)),
                  _const_spec(conv_w.shape), _const_spec((1, D_CONV)), _const_spec((1, D_CONV)),
                  _const_spec((1, D_CONV))],
        out_specs=[cols, rows(D_ATT, cur), cols, rows(D_CONV, lag), last(tt), last(tt),
                   last(HALO)],
        out_shape=[jax.ShapeDtypeStruct((b, n_t, D_ATT, tt), BF16),
                   jax.ShapeDtypeStruct((b, t, D_ATT), BF16),
                   jax.ShapeDtypeStruct((b, n_t, D_ATT, tt), BF16),
                   jax.ShapeDtypeStruct((b, t, D_CONV), BF16),
                   jax.ShapeDtypeStruct((b, tt, D_ATT), F32),
                   jax.ShapeDtypeStruct((b, tt, D_ATT), F32),
                   jax.ShapeDtypeStruct((b, HALO, D_CONV), F32)],
        scratch_shapes=[pltpu.VMEM((D_CONV // LANES, HALO + tt, LANES), F32),
                        pltpu.VMEM((tt, D_CONV), F32), pltpu.VMEM((tt, D_CONV), F32)],
        compiler_params=_params(2),
        name="mixer_in_prompt",
    )(x, g, w_in, conv_w, conv_b, cln_g, cln_b)


def _mixer_in_sample_kernel(x_ref, g_ref, w_ref, q_ref, k_ref, v_ref, u_ref):
    q, k, v, u, _ = _mixer_proj(x_ref, g_ref, w_ref)
    q_ref[...] = q.astype(BF16)
    k_ref[...] = k
    v_ref[...] = v
    u_ref[...] = u


def _mixer_in_sample(x, g, w_in):
    n, d = x.shape
    d_in = w_in.shape[1]
    full = lambda c: pl.BlockSpec((n, c), lambda i: (0, 0))
    return pl.pallas_call(
        _mixer_in_sample_kernel,
        grid=(1,),
        in_specs=[full(d), _const_spec((1, d)), _const_spec((d, d_in))],
        out_specs=[full(D_ATT), full(D_ATT), full(D_ATT), full(D_CONV)],
        out_shape=[jax.ShapeDtypeStruct((n, D_ATT), BF16),
                   jax.ShapeDtypeStruct((n, D_ATT), F32),
                   jax.ShapeDtypeStruct((n, D_ATT), F32),
                   jax.ShapeDtypeStruct((n, D_CONV), F32)],
        compiler_params=_params(1),
        name="mixer_in_sample",
    )(x, g, w_in)


def _stack_head_pair(q2):
    lane = lax.broadcasted_iota(jnp.int32, q2.shape, 1)
    zero = jnp.zeros_like(q2)
    return jnp.concatenate([jnp.where(lane < ATT_HEAD_DIM, q2, zero),
                            jnp.where(lane >= ATT_HEAD_DIM, q2, zero)], axis=0)


def _merge_head_pair(o):
    m = o.shape[0] // 2
    lane = lax.broadcasted_iota(jnp.int32, (m, LANES), 1)
    return jnp.where(lane < ATT_HEAD_DIM, o[:m], o[m:])


def _band_attn_kernel(qt_ref, kp_ref, kc_ref, vtp_ref, vtc_ref, bias_ref, o_ref, s_buf, p_buf):
    first = pl.program_id(1) == 0
    row = lax.broadcasted_iota(jnp.int32, (LANES, GROUP), 0)
    items = [(grp, p) for grp in range(ATT_STEP // GROUP) for p in range(ATT_HEADS // 2)]
    tiles_per_block = ATT_STEP // MXU_COLS
    n_lt = 2 * GROUP // LANES

    def tile_slice(tile):
        t = tile % tiles_per_block
        return slice(t * MXU_COLS, (t + 1) * MXU_COLS)

    def scores(n):
        grp, p = items[n]
        rows = slice(p * LANES, (p + 1) * LANES)
        qt2 = qt_ref[rows, grp * GROUP:(grp + 1) * GROUP]
        zero = jnp.zeros_like(qt2)
        qm = jnp.concatenate([jnp.where(row < ATT_HEAD_DIM, qt2, zero),
                              jnp.where(row >= ATT_HEAD_DIM, qt2, zero)], axis=1)
        for j in range(N_KEY_TILES):
            tile = grp + j
            k_ref = kp_ref if tile < tiles_per_block else kc_ref
            s = _dot(k_ref[tile_slice(tile), rows], qm)
            for lt in range(n_lt):
                s_buf[n % 2, lt, j * MXU_COLS:(j + 1) * MXU_COLS, :] = s[:, lt * LANES:(lt + 1) * LANES]

    def finish(n):
        grp, p = items[n]
        slot = n % 2
        rows = slice(p * LANES, (p + 1) * LANES)
        variant = jnp.where(first, 1 + grp, 0)
        inv_l = []
        for lt in range(n_lt):
            hh, cp = divmod(lt, GROUP // LANES)
            w0 = LANES * cp
            t = s_buf[slot, lt, w0:w0 + KEY_WINDOW, :] + bias_ref[variant, 2 * p + hh, cp]
            m = jnp.max(t, axis=0, keepdims=True)
            e = jnp.exp2(t - m)
            inv_l.append(1.0 / jnp.sum(e, axis=0, keepdims=True))
            p_buf[slot, lt, w0:w0 + KEY_WINDOW, :] = e.astype(BF16)
            z0 = KEY_WINDOW if cp == 0 else 0
            p_buf[slot, lt, z0:z0 + LANES, :] = jnp.zeros((LANES, LANES), BF16)
        o = None
        for j in range(N_KEY_TILES):
            tile = grp + j
            vt_ref = vtp_ref if tile < tiles_per_block else vtc_ref
            pt = jnp.concatenate([p_buf[slot, lt, j * MXU_COLS:(j + 1) * MXU_COLS, :]
                                  for lt in range(n_lt)], axis=1)
            d = _dot(vt_ref[rows, tile_slice(tile)], pt)
            o = d if o is None else o + d
        o = o * jnp.concatenate(inv_l, axis=1)
        ot = jnp.where(row < ATT_HEAD_DIM, o[:, 0:GROUP], o[:, GROUP:2 * GROUP])
        o_ref[grp * GROUP:(grp + 1) * GROUP, rows] = ot.T.astype(BF16)

    scores(0)
    for n in range(len(items)):
        if n + 1 < len(items):
            scores(n + 1)
        finish(n)


def _toeplitz_bias(rel_bias, n_rows, n_cols, shift):
    n_heads = rel_bias.shape[0]
    period = n_cols + n_rows
    e0 = shift - REL_CLIP
    pad = period + abs(e0)
    table = jnp.pad(rel_bias[:, ::-1].astype(F32), ((0, 0), (pad, pad)), mode="edge")
    base = pad - e0
    vec = jnp.concatenate([table[:, base:base + n_cols + 1],
                           table[:, base - (n_rows - 1):base]], axis=1)
    flat = jnp.tile(vec, (1, n_rows))[:, :n_rows * (period - 1)]
    return flat.reshape(n_heads, n_rows, period - 1)[:, :, :n_cols]


def _band_bias_prompt(rel_bias):
    base = _toeplitz_bias(rel_bias, CHUNK, KEY_WINDOW + CHUNK, ATT_PAST + CHUNK) * LOG2E
    base = jnp.swapaxes(base, 1, 2)
    window = jnp.concatenate([base[:, CHUNK:CHUNK + KEY_WINDOW], base[:, 0:KEY_WINDOW]], axis=2)

    n_cp = GROUP // LANES
    cp = np.arange(n_cp)[None, :, None, None]
    kk = LANES * cp + np.arange(KEY_WINDOW)[None, None, :, None]
    kj = kk - CHUNK * (2 * cp + np.arange(LANES)[None, None, None, :] // CHUNK)
    first_valid = np.array([0, ATT_PAST, ATT_PAST - GROUP])[:, None, None, None]
    valid = (kj >= 0) & (kj < BAND) & (kk >= first_valid)
    return jnp.where(valid[:, None], window[None, :, None], NEG_INF)


def _band_attn_prompt(qt, k, vt, bias):
    b, t, _ = k.shape
    prev = lambda g: jnp.maximum(g - 1, 0)
    cols = lambda f: pl.BlockSpec((None, None, D_ATT, ATT_STEP), lambda i, g: (i, f(g), 0, 0))
    rows = lambda f: pl.BlockSpec((None, ATT_STEP, D_ATT), lambda i, g: (i, f(g), 0))
    same = lambda g: g
    return pl.pallas_call(
        _band_attn_kernel,
        grid=(b, t // ATT_STEP),
        in_specs=[cols(same), rows(prev), rows(same), cols(prev), cols(same),
                  _const_spec(bias.shape)],
        out_specs=rows(same),
        out_shape=jax.ShapeDtypeStruct((b, t, D_ATT), BF16),
        scratch_shapes=[pltpu.VMEM((2, 2 * GROUP // LANES, N_KEY_TILES * MXU_COLS, LANES), F32),
                        pltpu.VMEM((2, 2 * GROUP // LANES, N_KEY_TILES * MXU_COLS, LANES), BF16)],
        compiler_params=_params(2),
        name="band_attn_prompt",
    )(qt, k, k, vt, vt, bias)


def _shift_in(old_ref, new_ref, t_s):
    past = old_ref.shape[1]
    lane = lax.broadcasted_iota(jnp.int32, (old_ref.shape[0], LANES), 1)
    rolled = [pltpu.roll(old_ref[:, j * LANES:(j + 1) * LANES], LANES - t_s, axis=1)
              for j in range(past // LANES)]
    tail = rolled[1:] + [new_ref[...]]
    return jnp.concatenate([jnp.where(lane < LANES - t_s, a, b) for a, b in zip(rolled, tail)],
                           axis=1)


def _band_attn_sample_kernel(q_ref, kn_ref, vn_ref, ck_ref, cv_ref, bias_ref, bias0_ref,
                             o_ref, nk_ref, nv_ref, *, t_s):
    nk = _shift_in(ck_ref, kn_ref, t_s)
    nv = _shift_in(cv_ref, vn_ref, t_s)
    nk_ref[...] = nk
    nv_ref[...] = nv
    keys = (ck_ref[:, 0:LANES].astype(BF16), nk.astype(BF16))
    vals = (cv_ref[:, 0:LANES].astype(BF16), nv.astype(BF16))
    for p in range(ATT_HEADS // 2):
        sl = slice(p * LANES, (p + 1) * LANES)
        qm = _stack_head_pair(q_ref[:, sl])
        s = [_dot(qm, kt[sl, :]) + jnp.concatenate([b[2 * p], b[2 * p + 1]], axis=0)
             for kt, b in zip(keys, (bias0_ref, bias_ref))]
        m = jnp.maximum(*[jnp.max(x, axis=-1, keepdims=True) for x in s])
        e = [jnp.exp2(x - m) for x in s]
        inv_l = 1.0 / sum(jnp.sum(x, axis=-1, keepdims=True) for x in e)
        o = sum(_dot_nt(x.astype(BF16), vt[sl, :]) for x, vt in zip(e, vals)) * inv_l
        o_ref[:, sl] = _merge_head_pair(o).astype(BF16)


def _band_attn_sample(q, kn_t, vn_t, cache_kt, cache_vt, rel_bias):
    b, t_s, _ = q.shape
    past = cache_kt.shape[2]
    bias = _toeplitz_bias(rel_bias, t_s, past, ATT_PAST - t_s) * LOG2E
    bias0 = jnp.where(jnp.arange(LANES) < t_s,
                      _toeplitz_bias(rel_bias, t_s, LANES, ATT_PAST) * LOG2E, NEG_INF)
    rows = pl.BlockSpec((None, t_s, D_ATT), lambda i: (i, 0, 0))
    new = pl.BlockSpec((None, D_ATT, LANES), lambda i: (i, 0, 0))
    buf = pl.BlockSpec((None, D_ATT, past), lambda i: (i, 0, 0))
    return pl.pallas_call(
        functools.partial(_band_attn_sample_kernel, t_s=t_s),
        grid=(b,),
        in_specs=[rows, new, new, buf, buf, _const_spec(bias.shape), _const_spec(bias0.shape)],
        out_specs=[rows, buf, buf],
        out_shape=[jax.ShapeDtypeStruct((b, t_s, D_ATT), BF16),
                   jax.ShapeDtypeStruct((b, D_ATT, past), F32),
                   jax.ShapeDtypeStruct((b, D_ATT, past), F32)],
        compiler_params=_params(1),
        name="band_attn_sample",
    )(q, kn_t, vn_t, cache_kt, cache_vt, bias, bias0)


def _out_proj_residual(att, c, x, wo_ref, gp_ref):
    y = _dot(att, wo_ref[0:D_ATT, :]) + _dot(c, wo_ref[D_ATT:D_ATT + D_CONV, :])
    return x + _rms(y, gp_ref[...])


def _mem_attn_heads(streams, a_buf):
    blocks = [(q, r0, mk, mv, hh) for q, r0, mk, mv in streams for hh in range(MEM_HEADS)]
    cols = lambda hh: slice(hh * MEM_HEAD_DIM, (hh + 1) * MEM_HEAD_DIM)
    scores = [_dot_nt(q[:, cols(hh)], mk(hh)) for q, _, mk, _, hh in blocks]
    probs = []
    for s in scores:
        e = jnp.exp(s - jnp.max(s, axis=-1, keepdims=True))
        probs.append((e.astype(BF16), 1.0 / jnp.sum(e, axis=-1, keepdims=True)))
    for (q, r0, _, mv, hh), (e, inv_l) in zip(blocks, probs):
        a_buf[r0:r0 + q.shape[0], cols(hh)] = (_dot(e, mv(hh)) * inv_l).astype(BF16)


def _stored_mem_head(ref, b, hh):
    halves = MEM_HEAD_DIM // LANES
    parts = [ref[b, pl.ds(dh * MEM_HEADS + hh, N_MEM, stride=halves * MEM_HEADS), :]
             for dh in range(halves)]
    return jnp.concatenate(parts, axis=1).astype(BF16)


def _mem_query(x, gpre_ref, wq_ref):
    h = _rms(x, gpre_ref[...]).astype(BF16)
    return (_dot(h, wq_ref[...]) * (MEM_HEAD_DIM ** -0.5)).astype(BF16)


def _sample_tail_kernel(att_ref, u_ref, halo_ref, x_ref, cw_ref, cb_ref, lg_ref, lb_ref,
                        wout_ref, gmix_ref, mk_ref, mv_ref, gpre_ref, wq_ref, wo_ref, gpost_ref,
                        o_ref, u_buf, conv_buf, c_buf, a_buf, *, t_s):
    n_streams = u_ref.shape[0]
    for b in range(n_streams):
        for c in range(D_CONV // LANES):
            sl = slice(c * LANES, (c + 1) * LANES)
            u_buf[c, b, 0:HALO, :] = halo_ref[b, :, sl]
            u_buf[c, b, HALO:HALO + t_s, :] = u_ref[b, :, sl]
        window = lambda c, start, b=b: u_buf[c, b, pl.ds(start, SUBLANES, stride=1), :]
        rows = pl.ds(b * t_s, t_s)
        _conv_module(window, cw_ref, cb_ref, lg_ref, lb_ref, conv_buf.at[rows], c_buf.at[rows], t_s)
    x = _out_proj_residual(att_ref[...], c_buf[...], x_ref[...], wout_ref, gmix_ref)
    q = _mem_query(x, gpre_ref, wq_ref)
    _mem_attn_heads([(q[b * t_s:(b + 1) * t_s, :], b * t_s,
                      functools.partial(_stored_mem_head, mk_ref, b),
                      functools.partial(_stored_mem_head, mv_ref, b))
                     for b in range(n_streams)], a_buf)
    o_ref[...] = x + _rms(_dot(a_buf[...], wo_ref[...]), gpost_ref[...])


def _sample_tail(att, u, halo, x, conv_w, conv_b, cln_g, cln_b, w_out, g_mix_post,
                 mk, mv, g_pre, w_mq, w_mo, g_post):
    b_s, t_s, _ = u.shape
    n, d = x.shape
    operands = (att, u, halo, x, conv_w, conv_b, cln_g, cln_b, w_out, g_mix_post,
                mk, mv, g_pre, w_mq, w_mo, g_post)
    return pl.pallas_call(
        functools.partial(_sample_tail_kernel, t_s=t_s),
        grid=(1,),
        in_specs=[_const_spec(a.shape) for a in operands],
        out_specs=pl.BlockSpec((n, d), lambda i: (0, 0)),
        out_shape=jax.ShapeDtypeStruct((n, d), F32),
        scratch_shapes=[pltpu.VMEM((D_CONV // LANES, b_s, HALO + t_s, LANES), F32),
                        pltpu.VMEM((n, D_CONV), F32), pltpu.VMEM((n, D_CONV), BF16),
                        pltpu.VMEM((n, d), BF16)],
        compiler_params=_params(1),
        name="sample_tail",
    )(*operands)


def _mix_mem_attn_kernel(att_ref, c_ref, x_ref, wout_ref, gmix_ref, mk_ref, mv_ref, gpre_ref,
                         wq_ref, wo_ref, gpost_ref, o_ref, a_buf):
    tt = x_ref.shape[0]
    halves = [slice(r0, r0 + tt // 2) for r0 in (0, tt // 2)]
    xs = [_out_proj_residual(att_ref[r, :], c_ref[r, :], x_ref[r, :], wout_ref, gmix_ref)
          for r in halves]
    qs = [_mem_query(x, gpre_ref, wq_ref) for x in xs]
    cols = lambda hh: slice(hh * MEM_HEAD_DIM, (hh + 1) * MEM_HEAD_DIM)
    mk = lambda hh: mk_ref[:, cols(hh)]
    mv = lambda hh: mv_ref[:, cols(hh)]
    _mem_attn_heads([(q, r.start, mk, mv) for q, r in zip(qs, halves)], a_buf)
    for r, x in zip(halves, xs):
        o_ref[r, :] = x + _rms(_dot(a_buf[r, :], wo_ref[...]), gpost_ref[...])


def _mix_mem_attn(att, c, x, w_out, g_mix_post, mk, mv, g_pre, w_mq, w_mo, g_post, *, tt):
    b, t, d = x.shape
    rows = lambda n: pl.BlockSpec((None, tt, n), lambda i, j: (i, j, 0))
    mem = pl.BlockSpec((None, N_MEM, d), lambda i, j: (i, 0, 0))
    return pl.pallas_call(
        _mix_mem_attn_kernel,
        grid=(b, t // tt),
        in_specs=[rows(D_ATT), rows(D_CONV), rows(d), _const_spec(w_out.shape), _const_spec((1, d)),
                  mem, mem, _const_spec((1, d)), _const_spec((d, d)), _const_spec((d, d)),
                  _const_spec((1, d))],
        out_specs=rows(d),
        out_shape=jax.ShapeDtypeStruct((b, t, d), F32),
        scratch_shapes=[pltpu.VMEM((tt, d), BF16)],
        compiler_params=_params(2),
        name="mix_mem_attn",
    )(att, c, x, w_out, g_mix_post, mk, mv, g_pre, w_mq, w_mo, g_post)


def _ffn_kernel(x_ref, gpre_ref, wg_ref, wu_ref, wd_ref, gpost_ref, o_ref, acc_ref):
    tt = x_ref.shape[0]
    halves = [slice(r0, r0 + tt // 2) for r0 in (0, tt // 2)]
    xs = [x_ref[rows, :] for rows in halves]
    hs = [_rms(x, gpre_ref[...]).astype(BF16) for x in xs]
    n_chunks = D_FF // FF_CHUNK
    cols = lambda c: slice(c * FF_CHUNK, (c + 1) * FF_CHUNK)
    gate_up = lambda c: [(_dot(h, wg_ref[:, cols(c)]), _dot(h, wu_ref[:, cols(c)])) for h in hs]
    nxt = gate_up(0)
    for c in range(n_chunks):
        cur = nxt
        if c + 1 < n_chunks:
            nxt = gate_up(c + 1)
        for rows, (gate, up) in zip(halves, cur):
            a = (gate * jax.nn.sigmoid(gate) * up).astype(BF16)
            part = _dot(a, wd_ref[cols(c), :])
            if c == 0:
                acc_ref[rows, :] = part
            else:
                acc_ref[rows, :] += part
    for rows, x in zip(halves, xs):
        o_ref[rows, :] = x + _rms(acc_ref[rows, :], gpost_ref[...])


def _ffn(x, g_pre, w_gate, w_up, w_down, g_post, *, tt):
    n, d = x.shape
    rows = pl.BlockSpec((tt, d), lambda i: (i, 0))
    return pl.pallas_call(
        _ffn_kernel,
        grid=(n // tt,),
        in_specs=[rows, _const_spec((1, d)), _const_spec((d, D_FF)), _const_spec((d, D_FF)),
                  _const_spec((D_FF, d)), _const_spec((1, d))],
        out_specs=rows,
        out_shape=jax.ShapeDtypeStruct((n, d), F32),
        scratch_shapes=[pltpu.VMEM((tt, d), F32)],
        compiler_params=_params(1),
        name="ffn",
    )(x, g_pre, w_gate, w_up, w_down, g_post)


def kernel(x_prompt, x_sample, cache_att_k, cache_att_v, cache_conv, cache_mem_k, cache_mem_v, mem_prompt, g_mix_pre, g_mix_post, w_in, rel_bias, conv_w, conv_b, cln_g, cln_b, w_out, g_mem_pre, g_mem_post, g_mem_kv, w_mq, w_mk, w_mv, w_mo, g_ffn_pre, g_ffn_post, w_gate, w_up, w_down):
    depth = w_in.shape[0]
    b, t_p, d = x_prompt.shape
    b_s, t_s, _ = x_sample.shape
    past = cache_att_k.shape[2]
    assert d == D_MODEL and t_p % ATT_PAST == 0 and past == ATT_PAST
    assert t_s <= CHUNK and t_s % 16 == 0 and CONV_WIDTH - 1 <= HALO

    row = lambda a: a[None, :]
    xp, xs = x_prompt, x_sample
    outs = [[] for _ in range(8)]
    for l in range(depth):
        bf = lambda w: w[l].astype(BF16)
        w_in_l, w_out_l = bf(w_in), bf(w_out)
        w_mq_l, w_mk_l, w_mv_l, w_mo_l = bf(w_mq), bf(w_mk), bf(w_mv), bf(w_mo)
        w_gate_l, w_up_l, w_down_l = bf(w_gate), bf(w_up), bf(w_down)
        conv = (conv_w[l], row(conv_b[l]), row(cln_g[l]), row(cln_b[l]))

        mk_p, mv_p, mk_pb, mv_pb = _mem_kv(mem_prompt, row(g_mem_kv[l]), w_mk_l, w_mv_l)
        qt, k, vt, c_act, k_last, v_last, u_last = _mixer_in_prompt(
            xp, row(g_mix_pre[l]), w_in_l, *conv)
        att = _band_attn_prompt(qt, k, vt, _band_bias_prompt(rel_bias[l]))
        mem_w = (row(g_mem_pre[l]), w_mq_l, w_mo_l, row(g_mem_post[l]))
        xp = _mix_mem_attn(att, c_act, xp, w_out_l, row(g_mix_post[l]), mk_pb, mv_pb, *mem_w,
                           tt=ATT_PAST)
        xp = _ffn(xp.reshape(b * t_p, d), row(g_ffn_pre[l]), w_gate_l, w_up_l, w_down_l,
                  row(g_ffn_post[l]), tt=FFN_ROWS).reshape(b, t_p, d)
        outs[0].append(k_last.reshape(b, ATT_PAST, ATT_HEADS, ATT_HEAD_DIM))
        outs[1].append(v_last.reshape(b, ATT_PAST, ATT_HEADS, ATT_HEAD_DIM))
        outs[2].append(u_last[:, HALO - (CONV_WIDTH - 1):])
        outs[3].append(mk_p.reshape(b, N_MEM, MEM_HEADS, MEM_HEAD_DIM))
        outs[4].append(mv_p.reshape(b, N_MEM, MEM_HEADS, MEM_HEAD_DIM))

        q, k_new, v_new, u = _mixer_in_sample(xs.reshape(b_s * t_s, d), row(g_mix_pre[l]), w_in_l)
        u = u.reshape(b_s, t_s, D_CONV)
        to_t = lambda a: jnp.swapaxes(a.reshape(b_s, -1, D_ATT), 1, 2)
        new_cols = lambda a: jnp.pad(to_t(a), ((0, 0), (0, 0), (LANES - t_s, 0)))
        att, nk, nv = _band_attn_sample(
            q.reshape(b_s, t_s, D_ATT), new_cols(k_new), new_cols(v_new),
            to_t(cache_att_k[l]), to_t(cache_att_v[l]), rel_bias[l])
        nk, nv = jnp.swapaxes(nk, 1, 2), jnp.swapaxes(nv, 1, 2)
        halo = jnp.pad(cache_conv[l], ((0, 0), (HALO - (CONV_WIDTH - 1), 0), (0, 0)))
        halves = MEM_HEAD_DIM // LANES
        stored = lambda c: jnp.swapaxes(c.reshape(b_s, N_MEM, MEM_HEADS, halves, LANES), 2, 3).reshape(
            b_s, N_MEM * halves * MEM_HEADS, LANES)
        xs = _sample_tail(att.reshape(b_s * t_s, D_ATT), u, halo, xs.reshape(b_s * t_s, d), *conv,
                          w_out_l, row(g_mix_post[l]), stored(cache_mem_k[l]),
                          stored(cache_mem_v[l]), *mem_w)
        xs = _ffn(xs, row(g_ffn_pre[l]), w_gate_l, w_up_l, w_down_l,
                  row(g_ffn_post[l]), tt=b_s * t_s).reshape(b_s, t_s, d)
        outs[5].append(nk.reshape(b_s, past, ATT_HEADS, ATT_HEAD_DIM))
        outs[6].append(nv.reshape(b_s, past, ATT_HEADS, ATT_HEAD_DIM))
        outs[7].append(jnp.concatenate([cache_conv[l], u], axis=1)[:, -(CONV_WIDTH - 1):])

    return (xp, xs) + tuple(jnp.stack(o, 0) for o in outs)
```

```python
import functools

import jax
import jax.numpy as jnp
import numpy as np
from jax import lax
from jax.experimental import pallas as pl
from jax.experimental.pallas import tpu as pltpu

D_MODEL = 1024
CHUNK = 64
ATT_PAST = 512
BAND = ATT_PAST + CHUNK
D_ATT = 512
ATT_HEADS = 8
ATT_HEAD_DIM = 64
D_CONV = 512
CONV_WIDTH = 31
REL_CLIP = 128
N_MEM = 256
MEM_HEADS = 4
MEM_HEAD_DIM = 256
D_FF = 2816
EPS = 1e-6
NEG_INF = -1e30

LANES = 128
SUBLANES = 8
MXU_COLS = 256
GROUP = 4 * CHUNK
N_KEY_TILES = (ATT_PAST + GROUP) // MXU_COLS
ATT_STEP = 2 * ATT_PAST
LOG2E = 1.4426950408889634
Q_SCALE = ATT_HEAD_DIM ** -0.5 * LOG2E
KEY_WINDOW = BAND + CHUNK
HALO = 32
FF_CHUNK = MXU_COLS
FFN_ROWS = 1024
SAMPLE_STREAMS_PER_STEP = 2
SAMPLE_TAIL_STEPS = 2
VMEM_LIMIT = 56 * 1024 * 1024

F32 = jnp.float32
BF16 = jnp.bfloat16


def _params(n_grid):
    return pltpu.CompilerParams(dimension_semantics=("arbitrary",) * n_grid,
                                vmem_limit_bytes=VMEM_LIMIT)


def _const_spec(shape):
    zeros = (0,) * len(shape)
    return pl.BlockSpec(shape, lambda *_: zeros, pipeline_mode=pl.Buffered(1))


def _dot(a, b):
    return jnp.dot(a, b, preferred_element_type=F32)


def _dot_nt(a, b):
    return lax.dot_general(a, b, (((1,), (1,)), ((), ())), preferred_element_type=F32)


def _rms(x, g):
    y = x * lax.rsqrt(jnp.mean(x * x, axis=-1, keepdims=True) + EPS)
    return y * g


def _mem_kv_kernel(m_ref, g_ref, wk_ref, wv_ref, k_ref, v_ref, kb_ref, vb_ref):
    m = _rms(m_ref[...], g_ref[...]).astype(BF16)
    k = _dot(m, wk_ref[...])
    v = _dot(m, wv_ref[...])
    k_ref[...] = k
    v_ref[...] = v
    kb_ref[...] = k.astype(BF16)
    vb_ref[...] = v.astype(BF16)


def _mem_kv(mem, g, wk, wv):
    b, n, d = mem.shape
    row = pl.BlockSpec((None, n, d), lambda i: (i, 0, 0))
    return pl.pallas_call(
        _mem_kv_kernel,
        grid=(b,),
        in_specs=[row, _const_spec((1, d)), _const_spec((d, d)), _const_spec((d, d))],
        out_specs=[row, row, row, row],
        out_shape=[jax.ShapeDtypeStruct((b, n, d), F32)] * 2
        + [jax.ShapeDtypeStruct((b, n, d), BF16)] * 2,
        compiler_params=_params(1),
        name="mem_kv",
    )(mem, g, wk, wv)


def _mixer_proj(x_ref, g_ref, w_ref):
    h = _rms(x_ref[...], g_ref[...]).astype(BF16)
    edges = (0, D_ATT, 2 * D_ATT, 3 * D_ATT, 3 * D_ATT + D_CONV, 3 * D_ATT + 2 * D_CONV)
    raw = [_dot(h, w_ref[:, a:b]) for a, b in zip(edges[:-1], edges[1:])]
    q, k, v, u_val, u_gate = raw
    return q * Q_SCALE, k, v, u_val * jax.nn.sigmoid(u_gate), raw


def _zero_after(x):
    bits = lax.bitcast_convert_type(x, jnp.uint32)
    zero = lax.shift_right_logical(lax.shift_right_logical(bits, jnp.uint32(16)), jnp.uint32(16))
    return lax.bitcast_convert_type(zero, F32)


def _conv_module(window, cw_ref, cb_ref, lg_ref, lb_ref, conv_buf, out_ref, tt, gate=None):
    lead = HALO - (CONV_WIDTH - 1)
    rs = min(tt, CHUNK)
    blocks = rs // SUBLANES
    strips = D_CONV // LANES
    for r0 in range(0, tt, rs):
        for c in range(strips):
            sl = slice(c * LANES, (c + 1) * LANES)
            taps = [jnp.broadcast_to(cw_ref[w:w + 1, sl], (SUBLANES, LANES))
                    for w in range(CONV_WIDTH)]
            bias = jnp.broadcast_to(cb_ref[:, sl], (SUBLANES, LANES))
            if gate is not None:
                bias = bias + gate((r0 // rs) * strips + c)
            accs = [bias] * blocks
            for d in range(SUBLANES * (blocks - 1) + CONV_WIDTH):
                win = window(c, r0 + lead + d)
                for b in range(blocks):
                    w = d - SUBLANES * b
                    if 0 <= w < CONV_WIDTH:
                        accs[b] = accs[b] + win * taps[w]
            for b in range(blocks):
                conv_buf[r0 + b * SUBLANES:r0 + (b + 1) * SUBLANES, sl] = accs[b]

        acc = conv_buf[r0:r0 + rs, :]
        mu = jnp.mean(acc, axis=-1, keepdims=True)
        xc = acc - mu
        var = jnp.mean(xc * xc, axis=-1, keepdims=True)
        y = xc * lax.rsqrt(var + EPS) * lg_ref[...] + lb_ref[...]
        out_ref[r0:r0 + rs, :] = (y * jax.nn.sigmoid(y)).astype(BF16)


def _mixer_in_prompt_kernel(x_ref, g_ref, w_ref, cw_ref, cb_ref, lg_ref, lb_ref,
                            qt_ref, k_ref, vt_ref, c_ref, kl_ref, vl_ref, ul_ref,
                            u_buf, u_next, conv_buf, *, tt):
    strips = D_CONV // LANES

    @pl.when(pl.program_id(1) == 0)
    def _():
        u_buf[...] = jnp.zeros(u_buf.shape, F32)

    q, k, v, u, raw = _mixer_proj(x_ref, g_ref, w_ref)
    qt_ref[...] = q.T.astype(BF16)
    k_ref[...] = k.astype(BF16)
    vt_ref[...] = v.T.astype(BF16)
    kl_ref[...] = k
    vl_ref[...] = v
    ul_ref[...] = u[tt - HALO:tt, :]
    u_next[...] = u

    n_units = (tt // CHUNK) * strips
    col_tiles = D_ATT // MXU_COLS
    n_pass = len(raw) * col_tiles

    def gate(n):
        pos = (n + 1) * n_pass / (n_units + 1)
        dot_i, nt = divmod(int(pos), col_tiles)
        r = int((pos - int(pos)) * (tt // SUBLANES)) * SUBLANES
        return _zero_after(raw[dot_i][r:r + SUBLANES, nt * MXU_COLS:nt * MXU_COLS + LANES])

    window = lambda c, start: u_buf[c, pl.ds(start, SUBLANES, stride=1), :]
    _conv_module(window, cw_ref, cb_ref, lg_ref, lb_ref, conv_buf, c_ref, tt, gate=gate)

    for c in range(strips):
        u_buf[c, 0:HALO, :] = u_buf[c, tt:tt + HALO, :]
        u_buf[c, HALO:HALO + tt, :] = u_next[:, c * LANES:(c + 1) * LANES]


def _mixer_in_prompt(x, g, w_in, conv_w, conv_b, cln_g, cln_b):
    b, t, d = x.shape
    tt = ATT_PAST
    n_t = t // tt
    d_in = w_in.shape[1]
    cur = lambda j: jnp.minimum(j, n_t - 1)
    lag = lambda j: jnp.maximum(j - 1, 0)
    rows = lambda c, f: pl.BlockSpec((None, tt, c), lambda i, j: (i, f(j), 0))
    cols = pl.BlockSpec((None, None, D_ATT, tt), lambda i, j: (i, cur(j), 0, 0))
    last = lambda r: pl.BlockSpec((None, r, D_ATT), lambda i, j: (i, 0, 0))
    return pl.pallas_call(
        functools.partial(_mixer_in_prompt_kernel, tt=tt),
        grid=(b, n_t + 1),
        in_specs=[rows(d, cur), _const_spec((1, d)), _const_spec((d, d_in)),
                  _const_spec(conv_w.shape), _const_spec((1, D_CONV)), _const_spec((1, D_CONV)),
                  _const_spec((1, D_CONV))],
        out_specs=[cols, rows(D_ATT, cur), cols, rows(D_CONV, lag), last(tt), last(tt),
                   last(HALO)],
        out_shape=[jax.ShapeDtypeStruct((b, n_t, D_ATT, tt), BF16),
                   jax.ShapeDtypeStruct((b, t, D_ATT), BF16),
                   jax.ShapeDtypeStruct((b, n_t, D_ATT, tt), BF16),
                   jax.ShapeDtypeStruct((b, t, D_CONV), BF16),
                   jax.ShapeDtypeStruct((b, tt, D_ATT), F32),
                   jax.ShapeDtypeStruct((b, tt, D_ATT), F32),
                   jax.ShapeDtypeStruct((b, HALO, D_CONV), F32)],
        scratch_shapes=[pltpu.VMEM((D_CONV // LANES, HALO + tt, LANES), F32),
                        pltpu.VMEM((tt, D_CONV), F32), pltpu.VMEM((tt, D_CONV), F32)],
        compiler_params=_params(2),
        name="mixer_in_prompt",
    )(x, g, w_in, conv_w, conv_b, cln_g, cln_b)


def _mixer_in_sample_kernel(x_ref, g_ref, w_ref, q_ref, k_ref, v_ref, u_ref):
    q, k, v, u, _ = _mixer_proj(x_ref, g_ref, w_ref)
    q_ref[...] = q.astype(BF16)
    k_ref[...] = k
    v_ref[...] = v
    u_ref[...] = u


def _mixer_in_sample(x, g, w_in):
    n, d = x.shape
    d_in = w_in.shape[1]
    full = lambda c: pl.BlockSpec((n, c), lambda i: (0, 0))
    return pl.pallas_call(
        _mixer_in_sample_kernel,
        grid=(1,),
        in_specs=[full(d), _const_spec((1, d)), _const_spec((d, d_in))],
        out_specs=[full(D_ATT), full(D_ATT), full(D_ATT), full(D_CONV)],
        out_shape=[jax.ShapeDtypeStruct((n, D_ATT), BF16),
                   jax.ShapeDtypeStruct((n, D_ATT), F32),
                   jax.ShapeDtypeStruct((n, D_ATT), F32),
                   jax.ShapeDtypeStruct((n, D_CONV), F32)],
        compiler_params=_params(1),
        name="mixer_in_sample",
    )(x, g, w_in)


def _stack_head_pair(q2):
    lane = lax.broadcasted_iota(jnp.int32, q2.shape, 1)
    zero = jnp.zeros_like(q2)
    return jnp.concatenate([jnp.where(lane < ATT_HEAD_DIM, q2, zero),
                            jnp.where(lane >= ATT_HEAD_DIM, q2, zero)], axis=0)


def _merge_head_pair(o):
    m = o.shape[0] // 2
    lane = lax.broadcasted_iota(jnp.int32, (m, LANES), 1)
    return jnp.where(lane < ATT_HEAD_DIM, o[:m], o[m:])


def _band_attn_kernel(qt_ref, kp_ref, kc_ref, vtp_ref, vtc_ref, bias_ref, o_ref, s_buf, p_buf):
    first = pl.program_id(1) == 0
    row = lax.broadcasted_iota(jnp.int32, (LANES, GROUP), 0)
    items = [(grp, p) for grp in range(ATT_STEP // GROUP) for p in range(ATT_HEADS // 2)]
    tiles_per_block = ATT_STEP // MXU_COLS
    n_lt = 2 * GROUP // LANES

    back = ATT_PAST // MXU_COLS

    def tile_slice(tile):
        t = tile % tiles_per_block
        return slice(t * MXU_COLS, (t + 1) * MXU_COLS)

    def in_tiles(ref, rows, col0, width):
        return ref[col0 // ATT_PAST, rows, col0 % ATT_PAST:col0 % ATT_PAST + width]

    @pl.when(first & (pl.program_id(0) == 0))
    def _():
        for lt in range(n_lt):
            z0 = KEY_WINDOW if lt % (GROUP // LANES) == 0 else 0
            p_buf[:, lt, z0:z0 + LANES, :] = jnp.zeros((2, LANES, LANES), BF16)

    def scores(n):
        grp, p = items[n]
        rows = slice(p * LANES, (p + 1) * LANES)
        qt2 = in_tiles(qt_ref, rows, grp * GROUP, GROUP)
        zero = jnp.zeros_like(qt2)
        qm = jnp.concatenate([jnp.where(row < ATT_HEAD_DIM, qt2, zero),
                              jnp.where(row >= ATT_HEAD_DIM, qt2, zero)], axis=1)
        for j in range(N_KEY_TILES):
            tile = tiles_per_block - back + grp + j
            k_ref = kp_ref if tile < tiles_per_block else kc_ref
            s = _dot(k_ref[tile_slice(tile), rows], qm)
            for lt in range(n_lt):
                w0 = LANES * (lt % (GROUP // LANES))
                lo, hi = max(j * MXU_COLS, w0), min((j + 1) * MXU_COLS, w0 + KEY_WINDOW)
                s_buf[n % 2, lt, lo:hi, :] = s[lo - j * MXU_COLS:hi - j * MXU_COLS,
                                               lt * LANES:(lt + 1) * LANES]

    def finish(n):
        grp, p = items[n]
        slot = n % 2
        rows = slice(p * LANES, (p + 1) * LANES)
        variant = jnp.where(first, 1 + grp, 0) if grp < back else 0
        for lt in range(n_lt):
            hh, cp = divmod(lt, GROUP // LANES)
            w0 = LANES * cp
            t = s_buf[slot, lt, w0:w0 + KEY_WINDOW, :] + bias_ref[variant, 2 * p + hh, cp]
            m = jnp.max(t, axis=0, keepdims=True)
            p_buf[slot, lt, w0:w0 + KEY_WINDOW, :] = jnp.exp2(t - m).astype(BF16)
        ones = jnp.ones((2 * SUBLANES, MXU_COLS), BF16)
        o = None
        for j in range(N_KEY_TILES):
            tile = tiles_per_block - back + grp + j
            vt_ref = vtp_ref if tile < tiles_per_block else vtc_ref
            pt = jnp.concatenate([p_buf[slot, lt, j * MXU_COLS:(j + 1) * MXU_COLS, :]
                                  for lt in range(n_lt)], axis=1)
            vt2 = in_tiles(vt_ref, rows, (tile % tiles_per_block) * MXU_COLS, MXU_COLS)
            d = _dot(jnp.concatenate([vt2, ones], axis=0), pt)
            o = d if o is None else o + d
        o = o[0:LANES] * (1.0 / o[LANES:LANES + 1])
        ot = jnp.where(row < ATT_HEAD_DIM, o[:, 0:GROUP], o[:, GROUP:2 * GROUP])
        o_ref[grp * GROUP:(grp + 1) * GROUP, rows] = ot.T.astype(BF16)

    scores(0)
    for n in range(len(items)):
        if n + 1 < len(items):
            scores(n + 1)
        finish(n)


def _toeplitz_bias(rel_bias, n_rows, n_cols, shift):
    n_heads = rel_bias.shape[0]
    period = n_cols + n_rows
    e0 = shift - REL_CLIP
    pad = period + abs(e0)
    table = jnp.pad(rel_bias[:, ::-1].astype(F32), ((0, 0), (pad, pad)), mode="edge")
    base = pad - e0
    vec = jnp.concatenate([table[:, base:base + n_cols + 1],
                           table[:, base - (n_rows - 1):base]], axis=1)
    flat = jnp.tile(vec, (1, n_rows))[:, :n_rows * (period - 1)]
    return flat.reshape(n_heads, n_rows, period - 1)[:, :, :n_cols]


def _band_bias_prompt(rel_bias):
    base = _toeplitz_bias(rel_bias, CHUNK, KEY_WINDOW + CHUNK, ATT_PAST + CHUNK) * LOG2E
    base = jnp.swapaxes(base, 1, 2)
    window = jnp.concatenate([base[:, CHUNK:CHUNK + KEY_WINDOW], base[:, 0:KEY_WINDOW]], axis=2)

    n_cp = GROUP // LANES
    cp = np.arange(n_cp)[None, :, None, None]
    kk = LANES * cp + np.arange(KEY_WINDOW)[None, None, :, None]
    kj = kk - CHUNK * (2 * cp + np.arange(LANES)[None, None, None, :] // CHUNK)
    first_valid = np.array([0, ATT_PAST, ATT_PAST - GROUP])[:, None, None, None]
    valid = (kj >= 0) & (kj < BAND) & (kk >= first_valid)
    return jnp.where(valid[:, None], window[None, :, None], NEG_INF)


def _band_attn_prompt(qt, k, vt, bias):
    b, t, _ = k.shape
    prev = lambda g: jnp.maximum(g - 1, 0)
    cols = lambda f: pl.BlockSpec((None, ATT_STEP // ATT_PAST, D_ATT, ATT_PAST),
                                  lambda i, g: (i, f(g), 0, 0))
    rows = lambda f: pl.BlockSpec((None, ATT_STEP, D_ATT), lambda i, g: (i, f(g), 0))
    same = lambda g: g
    return pl.pallas_call(
        _band_attn_kernel,
        grid=(b, t // ATT_STEP),
        in_specs=[cols(same), rows(prev), rows(same), cols(prev), cols(same),
                  _const_spec(bias.shape)],
        out_specs=rows(same),
        out_shape=jax.ShapeDtypeStruct((b, t, D_ATT), BF16),
        scratch_shapes=[pltpu.VMEM((2, 2 * GROUP // LANES, N_KEY_TILES * MXU_COLS, LANES), F32),
                        pltpu.VMEM((2, 2 * GROUP // LANES, N_KEY_TILES * MXU_COLS, LANES), BF16)],
        compiler_params=_params(2),
        name="band_attn_prompt",
    )(qt, k, k, vt, vt, bias)


def _shift_in(old_ref, new_ref, t_s):
    past = old_ref.shape[1]
    lane = lax.broadcasted_iota(jnp.int32, (old_ref.shape[0], LANES), 1)
    rolled = [pltpu.roll(old_ref[:, j * LANES:(j + 1) * LANES], LANES - t_s, axis=1)
              for j in range(past // LANES)]
    tail = rolled[1:] + [new_ref[...]]
    return jnp.concatenate([jnp.where(lane < LANES - t_s, a, b) for a, b in zip(rolled, tail)],
                           axis=1)


def _band_attn_sample_kernel(q_ref, kn_ref, vn_ref, ck_ref, cv_ref, bias_ref, bias0_ref,
                             o_ref, nk_ref, nv_ref, *, t_s):
    for s in range(q_ref.shape[0]):
        _band_attn_stream(q_ref.at[s], kn_ref.at[s], vn_ref.at[s], ck_ref.at[s], cv_ref.at[s],
                          bias_ref, bias0_ref, o_ref.at[s], nk_ref.at[s], nv_ref.at[s], t_s)


def _band_attn_stream(q_ref, kn_ref, vn_ref, ck_ref, cv_ref, bias_ref, bias0_ref,
                      o_ref, nk_ref, nv_ref, t_s):
    nk = _shift_in(ck_ref, kn_ref, t_s)
    nv = _shift_in(cv_ref, vn_ref, t_s)
    nk_ref[...] = nk
    nv_ref[...] = nv
    keys = (ck_ref[:, 0:LANES].astype(BF16), nk.astype(BF16))
    vals = (cv_ref[:, 0:LANES].astype(BF16), nv.astype(BF16))
    for p in range(ATT_HEADS // 2):
        sl = slice(p * LANES, (p + 1) * LANES)
        qm = _stack_head_pair(q_ref[:, sl])
        s = [_dot(qm, kt[sl, :]) + jnp.concatenate([b[2 * p], b[2 * p + 1]], axis=0)
             for kt, b in zip(keys, (bias0_ref, bias_ref))]
        m = jnp.maximum(*[jnp.max(x, axis=-1, keepdims=True) for x in s])
        e = [jnp.exp2(x - m) for x in s]
        inv_l = 1.0 / sum(jnp.sum(x, axis=-1, keepdims=True) for x in e)
        o = sum(_dot_nt(x.astype(BF16), vt[sl, :]) for x, vt in zip(e, vals)) * inv_l
        o_ref[:, sl] = _merge_head_pair(o).astype(BF16)


def _band_attn_sample(q, kn_t, vn_t, cache_kt, cache_vt, rel_bias):
    b, t_s, _ = q.shape
    past = cache_kt.shape[2]
    bias = _toeplitz_bias(rel_bias, t_s, past, ATT_PAST - t_s) * LOG2E
    bias0 = jnp.where(jnp.arange(LANES) < t_s,
                      _toeplitz_bias(rel_bias, t_s, LANES, ATT_PAST) * LOG2E, NEG_INF)
    per = SAMPLE_STREAMS_PER_STEP
    rows = pl.BlockSpec((per, t_s, D_ATT), lambda i: (i, 0, 0))
    new = pl.BlockSpec((per, D_ATT, LANES), lambda i: (i, 0, 0))
    buf = pl.BlockSpec((per, D_ATT, past), lambda i: (i, 0, 0))
    return pl.pallas_call(
        functools.partial(_band_attn_sample_kernel, t_s=t_s),
        grid=(b // per,),
        in_specs=[rows, new, new, buf, buf, _const_spec(bias.shape), _const_spec(bias0.shape)],
        out_specs=[rows, buf, buf],
        out_shape=[jax.ShapeDtypeStruct((b, t_s, D_ATT), BF16),
                   jax.ShapeDtypeStruct((b, D_ATT, past), F32),
                   jax.ShapeDtypeStruct((b, D_ATT, past), F32)],
        compiler_params=_params(1),
        name="band_attn_sample",
    )(q, kn_t, vn_t, cache_kt, cache_vt, bias, bias0)


def _out_proj_residual(att, c, x, wo_ref, gp_ref):
    y = _dot(att, wo_ref[0:D_ATT, :]) + _dot(c, wo_ref[D_ATT:D_ATT + D_CONV, :])
    return x + _rms(y, gp_ref[...])


def _mem_attn_heads(streams, a_buf):
    blocks = [(q, r0, mk, mv, hh) for q, r0, mk, mv in streams for hh in range(MEM_HEADS)]
    cols = lambda hh: slice(hh * MEM_HEAD_DIM, (hh + 1) * MEM_HEAD_DIM)
    scores = [_dot_nt(q[:, cols(hh)], mk(hh)) for q, _, mk, _, hh in blocks]
    probs = []
    for s in scores:
        e = jnp.exp(s - jnp.max(s, axis=-1, keepdims=True))
        probs.append((e.astype(BF16), 1.0 / jnp.sum(e, axis=-1, keepdims=True)))
    for (q, r0, _, mv, hh), (e, inv_l) in zip(blocks, probs):
        a_buf[r0:r0 + q.shape[0], cols(hh)] = (_dot(e, mv(hh)) * inv_l).astype(BF16)


def _stored_mem_head(ref, b, hh):
    halves = MEM_HEAD_DIM // LANES
    parts = [ref[b, pl.ds(dh * MEM_HEADS + hh, N_MEM, stride=halves * MEM_HEADS), :]
             for dh in range(halves)]
    return jnp.concatenate(parts, axis=1).astype(BF16)


def _mem_query(x, gpre_ref, wq_ref):
    h = _rms(x, gpre_ref[...]).astype(BF16)
    return (_dot(h, wq_ref[...]) * (MEM_HEAD_DIM ** -0.5)).astype(BF16)


def _sample_tail_kernel(att_ref, u_ref, halo_ref, x_ref, cw_ref, cb_ref, lg_ref, lb_ref,
                        wout_ref, gmix_ref, mk_ref, mv_ref, gpre_ref, wq_ref, wo_ref, gpost_ref,
                        o_ref, u_buf, conv_buf, c_buf, a_buf, *, t_s):
    n_streams = u_ref.shape[0]
    for b in range(n_streams):
        for c in range(D_CONV // LANES):
            sl = slice(c * LANES, (c + 1) * LANES)
            u_buf[c, b, 0:HALO, :] = halo_ref[b, :, sl]
            u_buf[c, b, HALO:HALO + t_s, :] = u_ref[b, :, sl]
        window = lambda c, start, b=b: u_buf[c, b, pl.ds(start, SUBLANES, stride=1), :]
        rows = pl.ds(b * t_s, t_s)
        _conv_module(window, cw_ref, cb_ref, lg_ref, lb_ref, conv_buf.at[rows], c_buf.at[rows], t_s)
    x = _out_proj_residual(att_ref[...], c_buf[...], x_ref[...], wout_ref, gmix_ref)
    q = _mem_query(x, gpre_ref, wq_ref)
    _mem_attn_heads([(q[b * t_s:(b + 1) * t_s, :], b * t_s,
                      functools.partial(_stored_mem_head, mk_ref, b),
                      functools.partial(_stored_mem_head, mv_ref, b))
                     for b in range(n_streams)], a_buf)
    o_ref[...] = x + _rms(_dot(a_buf[...], wo_ref[...]), gpost_ref[...])


def _sample_tail(att, u, halo, x, conv_w, conv_b, cln_g, cln_b, w_out, g_mix_post,
                 mk, mv, g_pre, w_mq, w_mo, g_post):
    b_s, t_s, _ = u.shape
    n, d = x.shape
    steps = SAMPLE_TAIL_STEPS
    sb, sn = b_s // steps, n // steps
    streams = lambda a: pl.BlockSpec((sb,) + a.shape[1:], lambda i: (i,) + (0,) * (a.ndim - 1))
    rows = lambda a: pl.BlockSpec((sn, a.shape[1]), lambda i: (i, 0))
    const = lambda a: _const_spec(a.shape)
    operands = (att, u, halo, x, conv_w, conv_b, cln_g, cln_b, w_out, g_mix_post,
                mk, mv, g_pre, w_mq, w_mo, g_post)
    specs = (rows, streams, streams, rows, const, const, const, const, const, const,
             streams, streams, const, const, const, const)
    return pl.pallas_call(
        functools.partial(_sample_tail_kernel, t_s=t_s),
        grid=(steps,),
        in_specs=[spec(a) for spec, a in zip(specs, operands)],
        out_specs=pl.BlockSpec((sn, d), lambda i: (i, 0)),
        out_shape=jax.ShapeDtypeStruct((n, d), F32),
        scratch_shapes=[pltpu.VMEM((D_CONV // LANES, sb, HALO + t_s, LANES), F32),
                        pltpu.VMEM((sn, D_CONV), F32), pltpu.VMEM((sn, D_CONV), BF16),
                        pltpu.VMEM((sn, d), BF16)],
        compiler_params=_params(1),
        name="sample_tail",
    )(*operands)


def _mix_mem_attn_kernel(att_ref, c_ref, x_ref, wout_ref, gmix_ref, mk_ref, mv_ref, gpre_ref,
                         wq_ref, wo_ref, gpost_ref, o_ref, a_buf):
    tt = x_ref.shape[0]
    halves = [slice(r0, r0 + tt // 2) for r0 in (0, tt // 2)]
    xs = [_out_proj_residual(att_ref[r, :], c_ref[r, :], x_ref[r, :], wout_ref, gmix_ref)
          for r in halves]
    qs = [_mem_query(x, gpre_ref, wq_ref) for x in xs]
    cols = lambda hh: slice(hh * MEM_HEAD_DIM, (hh + 1) * MEM_HEAD_DIM)
    mk = lambda hh: mk_ref[:, cols(hh)]
    mv = lambda hh: mv_ref[:, cols(hh)]
    _mem_attn_heads([(q, r.start, mk, mv) for q, r in zip(qs, halves)], a_buf)
    for r, x in zip(halves, xs):
        o_ref[r, :] = x + _rms(_dot(a_buf[r, :], wo_ref[...]), gpost_ref[...])


def _mix_mem_attn(att, c, x, w_out, g_mix_post, mk, mv, g_pre, w_mq, w_mo, g_post, *, tt):
    b, t, d = x.shape
    rows = lambda n: pl.BlockSpec((None, tt, n), lambda i, j: (i, j, 0))
    mem = pl.BlockSpec((None, N_MEM, d), lambda i, j: (i, 0, 0))
    return pl.pallas_call(
        _mix_mem_attn_kernel,
        grid=(b, t // tt),
        in_specs=[rows(D_ATT), rows(D_CONV), rows(d), _const_spec(w_out.shape), _const_spec((1, d)),
                  mem, mem, _const_spec((1, d)), _const_spec((d, d)), _const_spec((d, d)),
                  _const_spec((1, d))],
        out_specs=rows(d),
        out_shape=jax.ShapeDtypeStruct((b, t, d), F32),
        scratch_shapes=[pltpu.VMEM((tt, d), BF16)],
        compiler_params=_params(2),
        name="mix_mem_attn",
    )(att, c, x, w_out, g_mix_post, mk, mv, g_pre, w_mq, w_mo, g_post)


def _ffn_kernel(x_ref, gpre_ref, wg_ref, wu_ref, wd_ref, gpost_ref, o_ref, acc_ref):
    tt = x_ref.shape[0]
    halves = [slice(r0, r0 + tt // 2) for r0 in (0, tt // 2)]
    xs = [x_ref[rows, :] for rows in halves]
    hs = [_rms(x, gpre_ref[...]).astype(BF16) for x in xs]
    n_chunks = D_FF // FF_CHUNK
    cols = lambda c: slice(c * FF_CHUNK, (c + 1) * FF_CHUNK)
    gate_up = lambda c: [(_dot(h, wg_ref[:, cols(c)]), _dot(h, wu_ref[:, cols(c)])) for h in hs]
    nxt = gate_up(0)
    for c in range(n_chunks):
        cur = nxt
        if c + 1 < n_chunks:
            nxt = gate_up(c + 1)
        for rows, (gate, up) in zip(halves, cur):
            a = (gate * jax.nn.sigmoid(gate) * up).astype(BF16)
            part = _dot(a, wd_ref[cols(c), :])
            if c == 0:
                acc_ref[rows, :] = part
            else:
                acc_ref[rows, :] += part
    for rows, x in zip(halves, xs):
        o_ref[rows, :] = x + _rms(acc_ref[rows, :], gpost_ref[...])


def _ffn(x, g_pre, w_gate, w_up, w_down, g_post, *, tt):
    n, d = x.shape
    rows = pl.BlockSpec((tt, d), lambda i: (i, 0))
    return pl.pallas_call(
        _ffn_kernel,
        grid=(n // tt,),
        in_specs=[rows, _const_spec((1, d)), _const_spec((d, D_FF)), _const_spec((d, D_FF)),
                  _const_spec((D_FF, d)), _const_spec((1, d))],
        out_specs=rows,
        out_shape=jax.ShapeDtypeStruct((n, d), F32),
        scratch_shapes=[pltpu.VMEM((tt, d), F32)],
        compiler_params=_params(1),
        name="ffn",
    )(x, g_pre, w_gate, w_up, w_down, g_post)


def kernel(x_prompt, x_sample, cache_att_k, cache_att_v, cache_conv, cache_mem_k, cache_mem_v, mem_prompt, g_mix_pre, g_mix_post, w_in, rel_bias, conv_w, conv_b, cln_g, cln_b, w_out, g_mem_pre, g_mem_post, g_mem_kv, w_mq, w_mk, w_mv, w_mo, g_ffn_pre, g_ffn_post, w_gate, w_up, w_down):
    depth = w_in.shape[0]
    b, t_p, d = x_prompt.shape
    b_s, t_s, _ = x_sample.shape
    past = cache_att_k.shape[2]
    assert d == D_MODEL and t_p % ATT_STEP == 0 and t_p % FFN_ROWS == 0 and past == ATT_PAST
    assert t_s <= CHUNK and t_s % 16 == 0 and CONV_WIDTH - 1 <= HALO

    row = lambda a: a[None, :]
    xp, xs = x_prompt, x_sample
    outs = [[] for _ in range(8)]
    for l in range(depth):
        bf = lambda w: w[l].astype(BF16)
        w_in_l, w_out_l = bf(w_in), bf(w_out)
        w_mq_l, w_mk_l, w_mv_l, w_mo_l = bf(w_mq), bf(w_mk), bf(w_mv), bf(w_mo)
        w_gate_l, w_up_l, w_down_l = bf(w_gate), bf(w_up), bf(w_down)
        conv = (conv_w[l], row(conv_b[l]), row(cln_g[l]), row(cln_b[l]))

        mk_p, mv_p, mk_pb, mv_pb = _mem_kv(mem_prompt, row(g_mem_kv[l]), w_mk_l, w_mv_l)
        qt, k, vt, c_act, k_last, v_last, u_last = _mixer_in_prompt(
            xp, row(g_mix_pre[l]), w_in_l, *conv)
        att = _band_attn_prompt(qt, k, vt, _band_bias_prompt(rel_bias[l]))
        mem_w = (row(g_mem_pre[l]), w_mq_l, w_mo_l, row(g_mem_post[l]))
        xp = _mix_mem_attn(att, c_act, xp, w_out_l, row(g_mix_post[l]), mk_pb, mv_pb, *mem_w,
                           tt=ATT_PAST)
        xp = _ffn(xp.reshape(b * t_p, d), row(g_ffn_pre[l]), w_gate_l, w_up_l, w_down_l,
                  row(g_ffn_post[l]), tt=FFN_ROWS).reshape(b, t_p, d)
        outs[0].append(k_last.reshape(b, ATT_PAST, ATT_HEADS, ATT_HEAD_DIM))
        outs[1].append(v_last.reshape(b, ATT_PAST, ATT_HEADS, ATT_HEAD_DIM))
        outs[2].append(u_last[:, HALO - (CONV_WIDTH - 1):])
        outs[3].append(mk_p.reshape(b, N_MEM, MEM_HEADS, MEM_HEAD_DIM))
        outs[4].append(mv_p.reshape(b, N_MEM, MEM_HEADS, MEM_HEAD_DIM))

        q, k_new, v_new, u = _mixer_in_sample(xs.reshape(b_s * t_s, d), row(g_mix_pre[l]), w_in_l)
        u = u.reshape(b_s, t_s, D_CONV)
        to_t = lambda a: jnp.swapaxes(a.reshape(b_s, -1, D_ATT), 1, 2)
        new_cols = lambda a: jnp.pad(to_t(a), ((0, 0), (0, 0), (LANES - t_s, 0)))
        att, nk, nv = _band_attn_sample(
            q.reshape(b_s, t_s, D_ATT), new_cols(k_new), new_cols(v_new),
            to_t(cache_att_k[l]), to_t(cache_att_v[l]), rel_bias[l])
        nk, nv = jnp.swapaxes(nk, 1, 2), jnp.swapaxes(nv, 1, 2)
        halo = jnp.pad(cache_conv[l], ((0, 0), (HALO - (CONV_WIDTH - 1), 0), (0, 0)))
        halves = MEM_HEAD_DIM // LANES
        stored = lambda c: jnp.swapaxes(c.reshape(b_s, N_MEM, MEM_HEADS, halves, LANES), 2, 3).reshape(
            b_s, N_MEM * halves * MEM_HEADS, LANES)
        xs = _sample_tail(att.reshape(b_s * t_s, D_ATT), u, halo, xs.reshape(b_s * t_s, d), *conv,
                          w_out_l, row(g_mix_post[l]), stored(cache_mem_k[l]),
                          stored(cache_mem_v[l]), *mem_w)
        xs = _ffn(xs, row(g_ffn_pre[l]), w_gate_l, w_up_l, w_down_l,
                  row(g_ffn_post[l]), tt=b_s * t_s).reshape(b_s, t_s, d)
        outs[5].append(nk.reshape(b_s, past, ATT_HEADS, ATT_HEAD_DIM))
        outs[6].append(nv.reshape(b_s, past, ATT_HEADS, ATT_HEAD_DIM))
        outs[7].append(jnp.concatenate([cache_conv[l], u], axis=1)[:, -(CONV_WIDTH - 1):])

    return (xp, xs) + tuple(jnp.stack(o, 0) for o in outs)
```

```python
import functools

import jax
import jax.numpy as jnp
import numpy as np
from jax import lax
from jax.experimental import pallas as pl
from jax.experimental.pallas import tpu as pltpu

D_MODEL = 1024
CHUNK = 64
ATT_PAST = 512
BAND = ATT_PAST + CHUNK
D_ATT = 512
ATT_HEADS = 8
ATT_HEAD_DIM = 64
D_CONV = 512
CONV_WIDTH = 31
REL_CLIP = 128
N_MEM = 256
MEM_HEADS = 4
MEM_HEAD_DIM = 256
D_FF = 2816
EPS = 1e-6
NEG_INF = -1e30

LANES = 128
SUBLANES = 8
MXU_COLS = 256
GROUP = 4 * CHUNK
N_KEY_TILES = (ATT_PAST + GROUP) // MXU_COLS
ATT_STEP = 2 * ATT_PAST
LOG2E = 1.4426950408889634
Q_SCALE = ATT_HEAD_DIM ** -0.5 * LOG2E
KEY_WINDOW = BAND + CHUNK
HALO = 32
FF_CHUNK = MXU_COLS
FFN_ROWS = 1024
MIX_ROWS = 1024
SAMPLE_STREAMS_PER_STEP = 2
SAMPLE_TAIL_STEPS = 2
VMEM_LIMIT = 56 * 1024 * 1024

F32 = jnp.float32
BF16 = jnp.bfloat16


def _params(n_grid):
    return pltpu.CompilerParams(dimension_semantics=("arbitrary",) * n_grid,
                                vmem_limit_bytes=VMEM_LIMIT)


def _const_spec(shape):
    zeros = (0,) * len(shape)
    return pl.BlockSpec(shape, lambda *_: zeros, pipeline_mode=pl.Buffered(1))


def _dot(a, b):
    return jnp.dot(a, b, preferred_element_type=F32)


def _dot_nt(a, b):
    return lax.dot_general(a, b, (((1,), (1,)), ((), ())), preferred_element_type=F32)


def _rms(x, g):
    y = x * lax.rsqrt(jnp.mean(x * x, axis=-1, keepdims=True) + EPS)
    return y * g


def _mem_kv_kernel(m_ref, g_ref, wk_ref, wv_ref, k_ref, v_ref, kb_ref, vb_ref):
    m = _rms(m_ref[...], g_ref[...]).astype(BF16)
    k = _dot(m, wk_ref[...])
    v = _dot(m, wv_ref[...])
    k_ref[...] = k
    v_ref[...] = v
    kb_ref[...] = k.astype(BF16)
    vb_ref[...] = v.astype(BF16)


def _mem_kv(mem, g, wk, wv):
    b, n, d = mem.shape
    row = pl.BlockSpec((None, n, d), lambda i: (i, 0, 0))
    return pl.pallas_call(
        _mem_kv_kernel,
        grid=(b,),
        in_specs=[row, _const_spec((1, d)), _const_spec((d, d)), _const_spec((d, d))],
        out_specs=[row, row, row, row],
        out_shape=[jax.ShapeDtypeStruct((b, n, d), F32)] * 2
        + [jax.ShapeDtypeStruct((b, n, d), BF16)] * 2,
        compiler_params=_params(1),
        name="mem_kv",
    )(mem, g, wk, wv)


def _mixer_proj(x_ref, g_ref, w_ref):
    h = _rms(x_ref[...], g_ref[...]).astype(BF16)
    edges = (0, D_ATT, 2 * D_ATT, 3 * D_ATT, 3 * D_ATT + D_CONV, 3 * D_ATT + 2 * D_CONV)
    raw = [_dot(h, w_ref[:, a:b]) for a, b in zip(edges[:-1], edges[1:])]
    q, k, v, u_val, u_gate = raw
    return q * Q_SCALE, k, v, u_val * jax.nn.sigmoid(u_gate), raw


def _zero_after(x):
    bits = lax.bitcast_convert_type(x, jnp.uint32)
    zero = lax.shift_right_logical(lax.shift_right_logical(bits, jnp.uint32(16)), jnp.uint32(16))
    return lax.bitcast_convert_type(zero, F32)


def _conv_module(window, cw_ref, cb_ref, lg_ref, lb_ref, conv_buf, out_ref, tt, gate=None):
    lead = HALO - (CONV_WIDTH - 1)
    rs = min(tt, CHUNK)
    blocks = rs // SUBLANES
    strips = D_CONV // LANES
    for r0 in range(0, tt, rs):
        for c in range(strips):
            sl = slice(c * LANES, (c + 1) * LANES)
            taps = [jnp.broadcast_to(cw_ref[w:w + 1, sl], (SUBLANES, LANES))
                    for w in range(CONV_WIDTH)]
            bias = jnp.broadcast_to(cb_ref[:, sl], (SUBLANES, LANES))
            if gate is not None:
                bias = bias + gate((r0 // rs) * strips + c)
            accs = [bias] * blocks
            for d in range(SUBLANES * (blocks - 1) + CONV_WIDTH):
                win = window(c, r0 + lead + d)
                for b in range(blocks):
                    w = d - SUBLANES * b
                    if 0 <= w < CONV_WIDTH:
                        accs[b] = accs[b] + win * taps[w]
            for b in range(blocks):
                conv_buf[r0 + b * SUBLANES:r0 + (b + 1) * SUBLANES, sl] = accs[b]

        acc = conv_buf[r0:r0 + rs, :]
        mu = jnp.mean(acc, axis=-1, keepdims=True)
        xc = acc - mu
        var = jnp.mean(xc * xc, axis=-1, keepdims=True)
        y = xc * lax.rsqrt(var + EPS) * lg_ref[...] + lb_ref[...]
        out_ref[r0:r0 + rs, :] = (y * jax.nn.sigmoid(y)).astype(BF16)


def _mixer_in_prompt_kernel(x_ref, g_ref, w_ref, cw_ref, cb_ref, lg_ref, lb_ref,
                            qt_ref, k_ref, vt_ref, c_ref, kl_ref, vl_ref, ul_ref,
                            u_buf, u_next, conv_buf, *, tt):
    strips = D_CONV // LANES

    @pl.when(pl.program_id(1) == 0)
    def _():
        u_buf[...] = jnp.zeros(u_buf.shape, F32)

    q, k, v, u, raw = _mixer_proj(x_ref, g_ref, w_ref)
    qt_ref[...] = q.T.astype(BF16)
    k_ref[...] = k.astype(BF16)
    vt_ref[...] = v.T.astype(BF16)
    kl_ref[...] = k
    vl_ref[...] = v
    ul_ref[...] = u[tt - HALO:tt, :]
    u_next[...] = u

    n_units = (tt // CHUNK) * strips
    col_tiles = D_ATT // MXU_COLS
    n_pass = len(raw) * col_tiles

    def gate(n):
        pos = (n + 1) * n_pass / (n_units + 1)
        dot_i, nt = divmod(int(pos), col_tiles)
        r = int((pos - int(pos)) * (tt // SUBLANES)) * SUBLANES
        return _zero_after(raw[dot_i][r:r + SUBLANES, nt * MXU_COLS:nt * MXU_COLS + LANES])

    window = lambda c, start: u_buf[c, pl.ds(start, SUBLANES, stride=1), :]
    _conv_module(window, cw_ref, cb_ref, lg_ref, lb_ref, conv_buf, c_ref, tt, gate=gate)

    for c in range(strips):
        u_buf[c, 0:HALO, :] = u_buf[c, tt:tt + HALO, :]
        u_buf[c, HALO:HALO + tt, :] = u_next[:, c * LANES:(c + 1) * LANES]


def _mixer_in_prompt(x, g, w_in, conv_w, conv_b, cln_g, cln_b):
    b, t, d = x.shape
    tt = ATT_PAST
    n_t = t // tt
    d_in = w_in.shape[1]
    cur = lambda j: jnp.minimum(j, n_t - 1)
    lag = lambda j: jnp.maximum(j - 1, 0)
    rows = lambda c, f: pl.BlockSpec((None, tt, c), lambda i, j: (i, f(j), 0))
    cols = pl.BlockSpec((None, None, D_ATT, tt), lambda i, j: (i, cur(j), 0, 0))
    last = lambda r: pl.BlockSpec((None, r, D_ATT), lambda i, j: (i, 0, 0))
    return pl.pallas_call(
        functools.partial(_mixer_in_prompt_kernel, tt=tt),
        grid=(b, n_t + 1),
        in_specs=[rows(d, cur), _const_spec((1, d)), _const_spec((d, d_in)),
                  _const_spec(conv_w.shape), _const_spec((1, D_CONV)), _const_spec((1, D_CONV)),
                  _const_spec((1, D_CONV))],
        out_specs=[cols, rows(D_ATT, cur), cols, rows(D_CONV, lag), last(tt), last(tt),
                   last(HALO)],
        out_shape=[jax.ShapeDtypeStruct((b, n_t, D_ATT, tt), BF16),
                   jax.ShapeDtypeStruct((b, t, D_ATT), BF16),
                   jax.ShapeDtypeStruct((b, n_t, D_ATT, tt), BF16),
                   jax.ShapeDtypeStruct((b, t, D_CONV), BF16),
                   jax.ShapeDtypeStruct((b, tt, D_ATT), F32),
                   jax.ShapeDtypeStruct((b, tt, D_ATT), F32),
                   jax.ShapeDtypeStruct((b, HALO, D_CONV), F32)],
        scratch_shapes=[pltpu.VMEM((D_CONV // LANES, HALO + tt, LANES), F32),
                        pltpu.VMEM((tt, D_CONV), F32), pltpu.VMEM((tt, D_CONV), F32)],
        compiler_params=_params(2),
        name="mixer_in_prompt",
    )(x, g, w_in, conv_w, conv_b, cln_g, cln_b)


def _mixer_in_sample_kernel(x_ref, g_ref, w_ref, q_ref, k_ref, v_ref, u_ref):
    q, k, v, u, _ = _mixer_proj(x_ref, g_ref, w_ref)
    q_ref[...] = q.astype(BF16)
    k_ref[...] = k
    v_ref[...] = v
    u_ref[...] = u


def _mixer_in_sample(x, g, w_in):
    n, d = x.shape
    d_in = w_in.shape[1]
    full = lambda c: pl.BlockSpec((n, c), lambda i: (0, 0))
    return pl.pallas_call(
        _mixer_in_sample_kernel,
        grid=(1,),
        in_specs=[full(d), _const_spec((1, d)), _const_spec((d, d_in))],
        out_specs=[full(D_ATT), full(D_ATT), full(D_ATT), full(D_CONV)],
        out_shape=[jax.ShapeDtypeStruct((n, D_ATT), BF16),
                   jax.ShapeDtypeStruct((n, D_ATT), F32),
                   jax.ShapeDtypeStruct((n, D_ATT), F32),
                   jax.ShapeDtypeStruct((n, D_CONV), F32)],
        compiler_params=_params(1),
        name="mixer_in_sample",
    )(x, g, w_in)


def _stack_head_pair(q2):
    lane = lax.broadcasted_iota(jnp.int32, q2.shape, 1)
    zero = jnp.zeros_like(q2)
    return jnp.concatenate([jnp.where(lane < ATT_HEAD_DIM, q2, zero),
                            jnp.where(lane >= ATT_HEAD_DIM, q2, zero)], axis=0)


def _merge_head_pair(o):
    m = o.shape[0] // 2
    lane = lax.broadcasted_iota(jnp.int32, (m, LANES), 1)
    return jnp.where(lane < ATT_HEAD_DIM, o[:m], o[m:])


def _band_attn_kernel(qt_ref, kp_ref, kc_ref, vtp_ref, vtc_ref, bias_ref, o_ref, s_buf, p_buf):
    first = pl.program_id(1) == 0
    row = lax.broadcasted_iota(jnp.int32, (LANES, GROUP), 0)
    items = [(grp, p) for grp in range(ATT_STEP // GROUP) for p in range(ATT_HEADS // 2)]
    tiles_per_block = ATT_STEP // MXU_COLS
    n_lt = 2 * GROUP // LANES

    back = ATT_PAST // MXU_COLS

    def tile_slice(tile):
        t = tile % tiles_per_block
        return slice(t * MXU_COLS, (t + 1) * MXU_COLS)

    def in_tiles(ref, rows, col0, width):
        return ref[col0 // ATT_PAST, rows, col0 % ATT_PAST:col0 % ATT_PAST + width]

    @pl.when(first & (pl.program_id(0) == 0))
    def _():
        for lt in range(n_lt):
            z0 = KEY_WINDOW if lt % (GROUP // LANES) == 0 else 0
            p_buf[:, lt, z0:z0 + LANES, :] = jnp.zeros((2, LANES, LANES), BF16)

    def scores(n):
        grp, p = items[n]
        rows = slice(p * LANES, (p + 1) * LANES)
        qt2 = in_tiles(qt_ref, rows, grp * GROUP, GROUP)
        zero = jnp.zeros_like(qt2)
        qm = jnp.concatenate([jnp.where(row < ATT_HEAD_DIM, qt2, zero),
                              jnp.where(row >= ATT_HEAD_DIM, qt2, zero)], axis=1)
        for j in range(N_KEY_TILES):
            tile = tiles_per_block - back + grp + j
            k_ref = kp_ref if tile < tiles_per_block else kc_ref
            s = _dot(k_ref[tile_slice(tile), rows], qm)
            for lt in range(n_lt):
                w0 = LANES * (lt % (GROUP // LANES))
                lo, hi = max(j * MXU_COLS, w0), min((j + 1) * MXU_COLS, w0 + KEY_WINDOW)
                s_buf[n % 2, lt, lo:hi, :] = s[lo - j * MXU_COLS:hi - j * MXU_COLS,
                                               lt * LANES:(lt + 1) * LANES]

    def finish(n):
        grp, p = items[n]
        slot = n % 2
        rows = slice(p * LANES, (p + 1) * LANES)
        variant = jnp.where(first, 1 + grp, 0) if grp < back else 0
        for lt in range(n_lt):
            hh, cp = divmod(lt, GROUP // LANES)
            w0 = LANES * cp
            t = s_buf[slot, lt, w0:w0 + KEY_WINDOW, :] + bias_ref[variant, 2 * p + hh, cp]
            m = jnp.max(t, axis=0, keepdims=True)
            p_buf[slot, lt, w0:w0 + KEY_WINDOW, :] = jnp.exp2(t - m).astype(BF16)
        ones = jnp.ones((2 * SUBLANES, MXU_COLS), BF16)
        o = None
        for j in range(N_KEY_TILES):
            tile = tiles_per_block - back + grp + j
            vt_ref = vtp_ref if tile < tiles_per_block else vtc_ref
            pt = jnp.concatenate([p_buf[slot, lt, j * MXU_COLS:(j + 1) * MXU_COLS, :]
                                  for lt in range(n_lt)], axis=1)
            vt2 = in_tiles(vt_ref, rows, (tile % tiles_per_block) * MXU_COLS, MXU_COLS)
            d = _dot(jnp.concatenate([vt2, ones], axis=0), pt)
            o = d if o is None else o + d
        o = o[0:LANES] * (1.0 / o[LANES:LANES + 1])
        ot = jnp.where(row < ATT_HEAD_DIM, o[:, 0:GROUP], o[:, GROUP:2 * GROUP])
        o_ref[grp * GROUP:(grp + 1) * GROUP, rows] = ot.T.astype(BF16)

    scores(0)
    for n in range(len(items)):
        if n + 1 < len(items):
            scores(n + 1)
        finish(n)


def _toeplitz_bias(rel_bias, n_rows, n_cols, shift):
    n_heads = rel_bias.shape[0]
    period = n_cols + n_rows
    e0 = shift - REL_CLIP
    pad = period + abs(e0)
    table = jnp.pad(rel_bias[:, ::-1].astype(F32), ((0, 0), (pad, pad)), mode="edge")
    base = pad - e0
    vec = jnp.concatenate([table[:, base:base + n_cols + 1],
                           table[:, base - (n_rows - 1):base]], axis=1)
    flat = jnp.tile(vec, (1, n_rows))[:, :n_rows * (period - 1)]
    return flat.reshape(n_heads, n_rows, period - 1)[:, :, :n_cols]


def _band_bias_prompt(rel_bias):
    base = _toeplitz_bias(rel_bias, CHUNK, KEY_WINDOW + CHUNK, ATT_PAST + CHUNK) * LOG2E
    base = jnp.swapaxes(base, 1, 2)
    window = jnp.concatenate([base[:, CHUNK:CHUNK + KEY_WINDOW], base[:, 0:KEY_WINDOW]], axis=2)

    n_cp = GROUP // LANES
    cp = np.arange(n_cp)[None, :, None, None]
    kk = LANES * cp + np.arange(KEY_WINDOW)[None, None, :, None]
    kj = kk - CHUNK * (2 * cp + np.arange(LANES)[None, None, None, :] // CHUNK)
    first_valid = np.array([0, ATT_PAST, ATT_PAST - GROUP])[:, None, None, None]
    valid = (kj >= 0) & (kj < BAND) & (kk >= first_valid)
    return jnp.where(valid[:, None], window[None, :, None], NEG_INF)


def _band_attn_prompt(qt, k, vt, bias):
    b, t, _ = k.shape
    prev = lambda g: jnp.maximum(g - 1, 0)
    cols = lambda f: pl.BlockSpec((None, ATT_STEP // ATT_PAST, D_ATT, ATT_PAST),
                                  lambda i, g: (i, f(g), 0, 0))
    rows = lambda f: pl.BlockSpec((None, ATT_STEP, D_ATT), lambda i, g: (i, f(g), 0))
    same = lambda g: g
    return pl.pallas_call(
        _band_attn_kernel,
        grid=(b, t // ATT_STEP),
        in_specs=[cols(same), rows(prev), rows(same), cols(prev), cols(same),
                  _const_spec(bias.shape)],
        out_specs=rows(same),
        out_shape=jax.ShapeDtypeStruct((b, t, D_ATT), BF16),
        scratch_shapes=[pltpu.VMEM((2, 2 * GROUP // LANES, N_KEY_TILES * MXU_COLS, LANES), F32),
                        pltpu.VMEM((2, 2 * GROUP // LANES, N_KEY_TILES * MXU_COLS, LANES), BF16)],
        compiler_params=_params(2),
        name="band_attn_prompt",
    )(qt, k, k, vt, vt, bias)


def _shift_in(old_ref, new_ref, t_s):
    past = old_ref.shape[1]
    lane = lax.broadcasted_iota(jnp.int32, (old_ref.shape[0], LANES), 1)
    rolled = [pltpu.roll(old_ref[:, j * LANES:(j + 1) * LANES], LANES - t_s, axis=1)
              for j in range(past // LANES)]
    tail = rolled[1:] + [new_ref[...]]
    return jnp.concatenate([jnp.where(lane < LANES - t_s, a, b) for a, b in zip(rolled, tail)],
                           axis=1)


def _band_attn_sample_kernel(q_ref, kn_ref, vn_ref, ck_ref, cv_ref, bias_ref, bias0_ref,
                             o_ref, nk_ref, nv_ref, *, t_s):
    for s in range(q_ref.shape[0]):
        _band_attn_stream(q_ref.at[s], kn_ref.at[s], vn_ref.at[s], ck_ref.at[s], cv_ref.at[s],
                          bias_ref, bias0_ref, o_ref.at[s], nk_ref.at[s], nv_ref.at[s], t_s)


def _band_attn_stream(q_ref, kn_ref, vn_ref, ck_ref, cv_ref, bias_ref, bias0_ref,
                      o_ref, nk_ref, nv_ref, t_s):
    nk = _shift_in(ck_ref, kn_ref, t_s)
    nv = _shift_in(cv_ref, vn_ref, t_s)
    nk_ref[...] = nk
    nv_ref[...] = nv
    keys = (ck_ref[:, 0:LANES].astype(BF16), nk.astype(BF16))
    vals = (cv_ref[:, 0:LANES].astype(BF16), nv.astype(BF16))
    for p in range(ATT_HEADS // 2):
        sl = slice(p * LANES, (p + 1) * LANES)
        qm = _stack_head_pair(q_ref[:, sl])
        s = [_dot(qm, kt[sl, :]) + jnp.concatenate([b[2 * p], b[2 * p + 1]], axis=0)
             for kt, b in zip(keys, (bias0_ref, bias_ref))]
        m = jnp.maximum(*[jnp.max(x, axis=-1, keepdims=True) for x in s])
        e = [jnp.exp2(x - m) for x in s]
        inv_l = 1.0 / sum(jnp.sum(x, axis=-1, keepdims=True) for x in e)
        o = sum(_dot_nt(x.astype(BF16), vt[sl, :]) for x, vt in zip(e, vals)) * inv_l
        o_ref[:, sl] = _merge_head_pair(o).astype(BF16)


def _band_attn_sample(q, kn_t, vn_t, cache_kt, cache_vt, rel_bias):
    b, t_s, _ = q.shape
    past = cache_kt.shape[2]
    bias = _toeplitz_bias(rel_bias, t_s, past, ATT_PAST - t_s) * LOG2E
    bias0 = jnp.where(jnp.arange(LANES) < t_s,
                      _toeplitz_bias(rel_bias, t_s, LANES, ATT_PAST) * LOG2E, NEG_INF)
    per = SAMPLE_STREAMS_PER_STEP
    rows = pl.BlockSpec((per, t_s, D_ATT), lambda i: (i, 0, 0))
    new = pl.BlockSpec((per, D_ATT, LANES), lambda i: (i, 0, 0))
    buf = pl.BlockSpec((per, D_ATT, past), lambda i: (i, 0, 0))
    return pl.pallas_call(
        functools.partial(_band_attn_sample_kernel, t_s=t_s),
        grid=(b // per,),
        in_specs=[rows, new, new, buf, buf, _const_spec(bias.shape), _const_spec(bias0.shape)],
        out_specs=[rows, buf, buf],
        out_shape=[jax.ShapeDtypeStruct((b, t_s, D_ATT), BF16),
                   jax.ShapeDtypeStruct((b, D_ATT, past), F32),
                   jax.ShapeDtypeStruct((b, D_ATT, past), F32)],
        compiler_params=_params(1),
        name="band_attn_sample",
    )(q, kn_t, vn_t, cache_kt, cache_vt, bias, bias0)


def _out_proj_residual(att, c, x, wo_ref, gp_ref):
    y = _dot(att, wo_ref[0:D_ATT, :]) + _dot(c, wo_ref[D_ATT:D_ATT + D_CONV, :])
    return x + _rms(y, gp_ref[...])


def _mem_attn_heads(streams, a_buf):
    blocks = [(q, r0, mk, mv, hh) for q, r0, mk, mv in streams for hh in range(MEM_HEADS)]
    cols = lambda hh: slice(hh * MEM_HEAD_DIM, (hh + 1) * MEM_HEAD_DIM)
    scores = [_dot_nt(q[:, cols(hh)], mk(hh)) for q, _, mk, _, hh in blocks]
    probs = []
    for s in scores:
        e = jnp.exp(s - jnp.max(s, axis=-1, keepdims=True))
        probs.append((e.astype(BF16), 1.0 / jnp.sum(e, axis=-1, keepdims=True)))
    for (q, r0, _, mv, hh), (e, inv_l) in zip(blocks, probs):
        a_buf[r0:r0 + q.shape[0], cols(hh)] = (_dot(e, mv(hh)) * inv_l).astype(BF16)


def _stored_mem_head(ref, b, hh):
    halves = MEM_HEAD_DIM // LANES
    parts = [ref[b, pl.ds(dh * MEM_HEADS + hh, N_MEM, stride=halves * MEM_HEADS), :]
             for dh in range(halves)]
    return jnp.concatenate(parts, axis=1).astype(BF16)


def _mem_query(x, gpre_ref, wq_ref):
    h = _rms(x, gpre_ref[...]).astype(BF16)
    return (_dot(h, wq_ref[...]) * (MEM_HEAD_DIM ** -0.5)).astype(BF16)


def _sample_tail_kernel(att_ref, u_ref, halo_ref, x_ref, cw_ref, cb_ref, lg_ref, lb_ref,
                        wout_ref, gmix_ref, mk_ref, mv_ref, gpre_ref, wq_ref, wo_ref, gpost_ref,
                        o_ref, u_buf, conv_buf, c_buf, a_buf, *, t_s):
    n_streams = u_ref.shape[0]
    for b in range(n_streams):
        for c in range(D_CONV // LANES):
            sl = slice(c * LANES, (c + 1) * LANES)
            u_buf[c, b, 0:HALO, :] = halo_ref[b, :, sl]
            u_buf[c, b, HALO:HALO + t_s, :] = u_ref[b, :, sl]
        window = lambda c, start, b=b: u_buf[c, b, pl.ds(start, SUBLANES, stride=1), :]
        rows = pl.ds(b * t_s, t_s)
        _conv_module(window, cw_ref, cb_ref, lg_ref, lb_ref, conv_buf.at[rows], c_buf.at[rows], t_s)
    x = _out_proj_residual(att_ref[...], c_buf[...], x_ref[...], wout_ref, gmix_ref)
    q = _mem_query(x, gpre_ref, wq_ref)
    _mem_attn_heads([(q[b * t_s:(b + 1) * t_s, :], b * t_s,
                      functools.partial(_stored_mem_head, mk_ref, b),
                      functools.partial(_stored_mem_head, mv_ref, b))
                     for b in range(n_streams)], a_buf)
    o_ref[...] = x + _rms(_dot(a_buf[...], wo_ref[...]), gpost_ref[...])


def _sample_tail(att, u, halo, x, conv_w, conv_b, cln_g, cln_b, w_out, g_mix_post,
                 mk, mv, g_pre, w_mq, w_mo, g_post):
    b_s, t_s, _ = u.shape
    n, d = x.shape
    steps = SAMPLE_TAIL_STEPS
    sb, sn = b_s // steps, n // steps
    streams = lambda a: pl.BlockSpec((sb,) + a.shape[1:], lambda i: (i,) + (0,) * (a.ndim - 1))
    rows = lambda a: pl.BlockSpec((sn, a.shape[1]), lambda i: (i, 0))
    const = lambda a: _const_spec(a.shape)
    operands = (att, u, halo, x, conv_w, conv_b, cln_g, cln_b, w_out, g_mix_post,
                mk, mv, g_pre, w_mq, w_mo, g_post)
    specs = (rows, streams, streams, rows, const, const, const, const, const, const,
             streams, streams, const, const, const, const)
    return pl.pallas_call(
        functools.partial(_sample_tail_kernel, t_s=t_s),
        grid=(steps,),
        in_specs=[spec(a) for spec, a in zip(specs, operands)],
        out_specs=pl.BlockSpec((sn, d), lambda i: (i, 0)),
        out_shape=jax.ShapeDtypeStruct((n, d), F32),
        scratch_shapes=[pltpu.VMEM((D_CONV // LANES, sb, HALO + t_s, LANES), F32),
                        pltpu.VMEM((sn, D_CONV), F32), pltpu.VMEM((sn, D_CONV), BF16),
                        pltpu.VMEM((sn, d), BF16)],
        compiler_params=_params(1),
        name="sample_tail",
    )(*operands)


def _mix_mem_attn_kernel(att_ref, c_ref, x_ref, wout_ref, gmix_ref, mk_ref, mv_ref, gpre_ref,
                         wq_ref, wo_ref, gpost_ref, o_ref, a_buf):
    tt = x_ref.shape[0]
    halves = [slice(r0, r0 + tt // 2) for r0 in (0, tt // 2)]
    xs = [_out_proj_residual(att_ref[r, :], c_ref[r, :], x_ref[r, :], wout_ref, gmix_ref)
          for r in halves]
    qs = [_mem_query(x, gpre_ref, wq_ref) for x in xs]
    cols = lambda hh: slice(hh * MEM_HEAD_DIM, (hh + 1) * MEM_HEAD_DIM)
    mk = lambda hh: mk_ref[:, cols(hh)]
    mv = lambda hh: mv_ref[:, cols(hh)]
    _mem_attn_heads([(q, r.start, mk, mv) for q, r in zip(qs, halves)], a_buf)
    for r, x in zip(halves, xs):
        o_ref[r, :] = x + _rms(_dot(a_buf[r, :], wo_ref[...]), gpost_ref[...])


def _mix_mem_attn(att, c, x, w_out, g_mix_post, mk, mv, g_pre, w_mq, w_mo, g_post, *, tt):
    b, t, d = x.shape
    rows = lambda n: pl.BlockSpec((None, tt, n), lambda i, j: (i, j, 0))
    mem = pl.BlockSpec((None, N_MEM, d), lambda i, j: (i, 0, 0))
    return pl.pallas_call(
        _mix_mem_attn_kernel,
        grid=(b, t // tt),
        in_specs=[rows(D_ATT), rows(D_CONV), rows(d), _const_spec(w_out.shape), _const_spec((1, d)),
                  mem, mem, _const_spec((1, d)), _const_spec((d, d)), _const_spec((d, d)),
                  _const_spec((1, d))],
        out_specs=rows(d),
        out_shape=jax.ShapeDtypeStruct((b, t, d), F32),
        scratch_shapes=[pltpu.VMEM((tt, d), BF16)],
        compiler_params=_params(2),
        name="mix_mem_attn",
    )(att, c, x, w_out, g_mix_post, mk, mv, g_pre, w_mq, w_mo, g_post)


def _ffn_kernel(x_ref, gpre_ref, wg_ref, wu_ref, wd_ref, gpost_ref, o_ref, acc_ref):
    tt = x_ref.shape[0]
    halves = [slice(r0, r0 + tt // 2) for r0 in (0, tt // 2)]
    xs = [x_ref[rows, :] for rows in halves]
    hs = [_rms(x, gpre_ref[...]).astype(BF16) for x in xs]
    n_chunks = D_FF // FF_CHUNK
    cols = lambda c: slice(c * FF_CHUNK, (c + 1) * FF_CHUNK)
    gate_up = lambda c: [(_dot(h, wg_ref[:, cols(c)]), _dot(h, wu_ref[:, cols(c)])) for h in hs]
    nxt = gate_up(0)
    for c in range(n_chunks):
        cur = nxt
        if c + 1 < n_chunks:
            nxt = gate_up(c + 1)
        for rows, (gate, up) in zip(halves, cur):
            a = (gate * jax.nn.sigmoid(gate) * up).astype(BF16)
            part = _dot(a, wd_ref[cols(c), :])
            if c == 0:
                acc_ref[rows, :] = part
            else:
                acc_ref[rows, :] += part
    for rows, x in zip(halves, xs):
        o_ref[rows, :] = x + _rms(acc_ref[rows, :], gpost_ref[...])


def _ffn(x, g_pre, w_gate, w_up, w_down, g_post, *, tt):
    n, d = x.shape
    rows = pl.BlockSpec((tt, d), lambda i: (i, 0))
    return pl.pallas_call(
        _ffn_kernel,
        grid=(n // tt,),
        in_specs=[rows, _const_spec((1, d)), _const_spec((d, D_FF)), _const_spec((d, D_FF)),
                  _const_spec((D_FF, d)), _const_spec((1, d))],
        out_specs=rows,
        out_shape=jax.ShapeDtypeStruct((n, d), F32),
        scratch_shapes=[pltpu.VMEM((tt, d), F32)],
        compiler_params=_params(1),
        name="ffn",
    )(x, g_pre, w_gate, w_up, w_down, g_post)


def kernel(x_prompt, x_sample, cache_att_k, cache_att_v, cache_conv, cache_mem_k, cache_mem_v, mem_prompt, g_mix_pre, g_mix_post, w_in, rel_bias, conv_w, conv_b, cln_g, cln_b, w_out, g_mem_pre, g_mem_post, g_mem_kv, w_mq, w_mk, w_mv, w_mo, g_ffn_pre, g_ffn_post, w_gate, w_up, w_down):
    depth = w_in.shape[0]
    b, t_p, d = x_prompt.shape
    b_s, t_s, _ = x_sample.shape
    past = cache_att_k.shape[2]
    assert d == D_MODEL and past == ATT_PAST
    assert t_p % ATT_STEP == 0 and t_p % MIX_ROWS == 0 and (b * t_p) % FFN_ROWS == 0
    assert t_s <= CHUNK and t_s % 16 == 0 and CONV_WIDTH - 1 <= HALO

    row = lambda a: a[None, :]
    xp, xs = x_prompt, x_sample
    outs = [[] for _ in range(8)]
    for l in range(depth):
        bf = lambda w: w[l].astype(BF16)
        w_in_l, w_out_l = bf(w_in), bf(w_out)
        w_mq_l, w_mk_l, w_mv_l, w_mo_l = bf(w_mq), bf(w_mk), bf(w_mv), bf(w_mo)
        w_gate_l, w_up_l, w_down_l = bf(w_gate), bf(w_up), bf(w_down)
        conv = (conv_w[l], row(conv_b[l]), row(cln_g[l]), row(cln_b[l]))

        mk_p, mv_p, mk_pb, mv_pb = _mem_kv(mem_prompt, row(g_mem_kv[l]), w_mk_l, w_mv_l)
        qt, k, vt, c_act, k_last, v_last, u_last = _mixer_in_prompt(
            xp, row(g_mix_pre[l]), w_in_l, *conv)
        att = _band_attn_prompt(qt, k, vt, _band_bias_prompt(rel_bias[l]))
        mem_w = (row(g_mem_pre[l]), w_mq_l, w_mo_l, row(g_mem_post[l]))
        xp = _mix_mem_attn(att, c_act, xp, w_out_l, row(g_mix_post[l]), mk_pb, mv_pb, *mem_w,
                           tt=MIX_ROWS)
        xp = _ffn(xp.reshape(b * t_p, d), row(g_ffn_pre[l]), w_gate_l, w_up_l, w_down_l,
                  row(g_ffn_post[l]), tt=FFN_ROWS).reshape(b, t_p, d)
        outs[0].append(k_last.reshape(b, ATT_PAST, ATT_HEADS, ATT_HEAD_DIM))
        outs[1].append(v_last.reshape(b, ATT_PAST, ATT_HEADS, ATT_HEAD_DIM))
        outs[2].append(u_last[:, HALO - (CONV_WIDTH - 1):])
        outs[3].append(mk_p.reshape(b, N_MEM, MEM_HEADS, MEM_HEAD_DIM))
        outs[4].append(mv_p.reshape(b, N_MEM, MEM_HEADS, MEM_HEAD_DIM))

        q, k_new, v_new, u = _mixer_in_sample(xs.reshape(b_s * t_s, d), row(g_mix_pre[l]), w_in_l)
        u = u.reshape(b_s, t_s, D_CONV)
        to_t = lambda a: jnp.swapaxes(a.reshape(b_s, -1, D_ATT), 1, 2)
        new_cols = lambda a: jnp.pad(to_t(a), ((0, 0), (0, 0), (LANES - t_s, 0)))
        att, nk, nv = _band_attn_sample(
            q.reshape(b_s, t_s, D_ATT), new_cols(k_new), new_cols(v_new),
            to_t(cache_att_k[l]), to_t(cache_att_v[l]), rel_bias[l])
        nk, nv = jnp.swapaxes(nk, 1, 2), jnp.swapaxes(nv, 1, 2)
        halo = jnp.pad(cache_conv[l], ((0, 0), (HALO - (CONV_WIDTH - 1), 0), (0, 0)))
        halves = MEM_HEAD_DIM // LANES
        stored = lambda c: jnp.swapaxes(c.reshape(b_s, N_MEM, MEM_HEADS, halves, LANES), 2, 3).reshape(
            b_s, N_MEM * halves * MEM_HEADS, LANES)
        xs = _sample_tail(att.reshape(b_s * t_s, D_ATT), u, halo, xs.reshape(b_s * t_s, d), *conv,
                          w_out_l, row(g_mix_post[l]), stored(cache_mem_k[l]),
                          stored(cache_mem_v[l]), *mem_w)
        xs = _ffn(xs, row(g_ffn_pre[l]), w_gate_l, w_up_l, w_down_l,
                  row(g_ffn_post[l]), tt=b_s * t_s).reshape(b_s, t_s, d)
        outs[5].append(nk.reshape(b_s, past, ATT_HEADS, ATT_HEAD_DIM))
        outs[6].append(nv.reshape(b_s, past, ATT_HEADS, ATT_HEAD_DIM))
        outs[7].append(jnp.concatenate([cache_conv[l], u], axis=1)[:, -(CONV_WIDTH - 1):])

    return (xp, xs) + tuple(jnp.stack(o, 0) for o in outs)
```

```python
import functools

import jax
import jax.numpy as jnp
import numpy as np
from jax import lax
from jax.experimental import pallas as pl
from jax.experimental.pallas import tpu as pltpu

D_MODEL = 1024
CHUNK = 64
ATT_PAST = 512
BAND = ATT_PAST + CHUNK
D_ATT = 512
ATT_HEADS = 8
ATT_HEAD_DIM = 64
D_CONV = 512
CONV_WIDTH = 31
REL_CLIP = 128
N_MEM = 256
MEM_HEADS = 4
MEM_HEAD_DIM = 256
D_FF = 2816
EPS = 1e-6
NEG_INF = -1e30

LANES = 128
SUBLANES = 8
MXU_COLS = 256
GROUP = 4 * CHUNK
N_KEY_TILES = (ATT_PAST + GROUP) // MXU_COLS
ATT_STEP = 4 * ATT_PAST
LOG2E = 1.4426950408889634
Q_SCALE = ATT_HEAD_DIM ** -0.5 * LOG2E
KEY_WINDOW = BAND + CHUNK
HALO = 32
FF_CHUNK = MXU_COLS
FFN_ROWS = 1024
MIX_ROWS = 1024
SAMPLE_STREAMS_PER_STEP = 2
SAMPLE_TAIL_STEPS = 2
VMEM_LIMIT = 56 * 1024 * 1024

F32 = jnp.float32
BF16 = jnp.bfloat16


def _params(n_grid):
    return pltpu.CompilerParams(dimension_semantics=("arbitrary",) * n_grid,
                                vmem_limit_bytes=VMEM_LIMIT)


def _const_spec(shape):
    zeros = (0,) * len(shape)
    return pl.BlockSpec(shape, lambda *_: zeros, pipeline_mode=pl.Buffered(1))


def _dot(a, b):
    return jnp.dot(a, b, preferred_element_type=F32)


def _dot_nt(a, b):
    return lax.dot_general(a, b, (((1,), (1,)), ((), ())), preferred_element_type=F32)


def _rms(x, g):
    y = x * lax.rsqrt(jnp.mean(x * x, axis=-1, keepdims=True) + EPS)
    return y * g


def _mem_kv_kernel(m_ref, g_ref, wk_ref, wv_ref, k_ref, v_ref, kb_ref, vb_ref):
    m = _rms(m_ref[...], g_ref[...]).astype(BF16)
    k = _dot(m, wk_ref[...].astype(BF16))
    v = _dot(m, wv_ref[...].astype(BF16))
    k_ref[...] = k
    v_ref[...] = v
    kb_ref[...] = k.astype(BF16)
    vb_ref[...] = v.astype(BF16)


def _mem_kv(mem, g, wk, wv):
    b, n, d = mem.shape
    rows = pl.BlockSpec((b * n, d), lambda i: (0, 0))
    outs = pl.pallas_call(
        _mem_kv_kernel,
        grid=(1,),
        in_specs=[rows, _const_spec((1, d)), _const_spec((d, d)), _const_spec((d, d))],
        out_specs=[rows, rows, rows, rows],
        out_shape=[jax.ShapeDtypeStruct((b * n, d), F32)] * 2
        + [jax.ShapeDtypeStruct((b * n, d), BF16)] * 2,
        compiler_params=_params(1),
        name="mem_kv",
    )(mem.reshape(b * n, d), g, wk, wv)
    return [o.reshape(b, n, d) for o in outs]


def _mixer_proj(x_ref, g_ref, w_ref):
    h = _rms(x_ref[...], g_ref[...]).astype(BF16)
    edges = (0, D_ATT, 2 * D_ATT, 3 * D_ATT, 3 * D_ATT + D_CONV, 3 * D_ATT + 2 * D_CONV)
    raw = [_dot(h, w_ref[:, a:b]) for a, b in zip(edges[:-1], edges[1:])]
    q, k, v, u_val, u_gate = raw
    return q * Q_SCALE, k, v, u_val * jax.nn.sigmoid(u_gate), raw


def _zero_after(x):
    bits = lax.bitcast_convert_type(x, jnp.uint32)
    zero = lax.shift_right_logical(lax.shift_right_logical(bits, jnp.uint32(16)), jnp.uint32(16))
    return lax.bitcast_convert_type(zero, F32)


def _conv_module(window, cw_ref, cb_ref, lg_ref, lb_ref, conv_buf, out_ref, tt, gate=None):
    lead = HALO - (CONV_WIDTH - 1)
    rs = min(tt, CHUNK)
    blocks = rs // SUBLANES
    strips = D_CONV // LANES
    for r0 in range(0, tt, rs):
        for c in range(strips):
            sl = slice(c * LANES, (c + 1) * LANES)
            taps = [jnp.broadcast_to(cw_ref[w:w + 1, sl], (SUBLANES, LANES))
                    for w in range(CONV_WIDTH)]
            bias = jnp.broadcast_to(cb_ref[:, sl], (SUBLANES, LANES))
            if gate is not None:
                bias = bias + gate((r0 // rs) * strips + c)
            accs = [bias] * blocks
            for d in range(SUBLANES * (blocks - 1) + CONV_WIDTH):
                win = window(c, r0 + lead + d)
                for b in range(blocks):
                    w = d - SUBLANES * b
                    if 0 <= w < CONV_WIDTH:
                        accs[b] = accs[b] + win * taps[w]
            for b in range(blocks):
                conv_buf[r0 + b * SUBLANES:r0 + (b + 1) * SUBLANES, sl] = accs[b]

        acc = conv_buf[r0:r0 + rs, :]
        mu = jnp.mean(acc, axis=-1, keepdims=True)
        xc = acc - mu
        var = jnp.mean(xc * xc, axis=-1, keepdims=True)
        y = xc * lax.rsqrt(var + EPS) * lg_ref[...] + lb_ref[...]
        out_ref[r0:r0 + rs, :] = (y * jax.nn.sigmoid(y)).astype(BF16)


def _mixer_in_prompt_kernel(x_ref, g_ref, w_ref, cw_ref, cb_ref, lg_ref, lb_ref,
                            qt_ref, k_ref, vt_ref, c_ref, kl_ref, vl_ref, ul_ref,
                            u_buf, u_next, conv_buf, *, tt):
    strips = D_CONV // LANES

    @pl.when(pl.program_id(1) == 0)
    def _():
        u_buf[...] = jnp.zeros(u_buf.shape, F32)

    q, k, v, u, raw = _mixer_proj(x_ref, g_ref, w_ref)
    qt_ref[...] = q.T.astype(BF16)
    k_ref[...] = k.astype(BF16)
    vt_ref[...] = v.T.astype(BF16)
    kl_ref[...] = k
    vl_ref[...] = v
    ul_ref[...] = u[tt - HALO:tt, :]
    u_next[...] = u

    n_units = (tt // CHUNK) * strips
    col_tiles = D_ATT // MXU_COLS
    n_pass = len(raw) * col_tiles

    def gate(n):
        pos = (n + 1) * n_pass / (n_units + 1)
        dot_i, nt = divmod(int(pos), col_tiles)
        r = int((pos - int(pos)) * (tt // SUBLANES)) * SUBLANES
        return _zero_after(raw[dot_i][r:r + SUBLANES, nt * MXU_COLS:nt * MXU_COLS + LANES])

    window = lambda c, start: u_buf[c, pl.ds(start, SUBLANES, stride=1), :]
    _conv_module(window, cw_ref, cb_ref, lg_ref, lb_ref, conv_buf, c_ref, tt, gate=gate)

    for c in range(strips):
        u_buf[c, 0:HALO, :] = u_buf[c, tt:tt + HALO, :]
        u_buf[c, HALO:HALO + tt, :] = u_next[:, c * LANES:(c + 1) * LANES]


def _mixer_in_prompt(x, g, w_in, conv_w, conv_b, cln_g, cln_b):
    b, t, d = x.shape
    tt = ATT_PAST
    n_t = t // tt
    d_in = w_in.shape[1]
    cur = lambda j: jnp.minimum(j, n_t - 1)
    lag = lambda j: jnp.maximum(j - 1, 0)
    rows = lambda c, f: pl.BlockSpec((None, tt, c), lambda i, j: (i, f(j), 0))
    cols = pl.BlockSpec((None, None, D_ATT, tt), lambda i, j: (i, cur(j), 0, 0))
    last = lambda r: pl.BlockSpec((None, r, D_ATT), lambda i, j: (i, 0, 0))
    return pl.pallas_call(
        functools.partial(_mixer_in_prompt_kernel, tt=tt),
        grid=(b, n_t + 1),
        in_specs=[rows(d, cur), _const_spec((1, d)), _const_spec((d, d_in)),
                  _const_spec(conv_w.shape), _const_spec((1, D_CONV)), _const_spec((1, D_CONV)),
                  _const_spec((1, D_CONV))],
        out_specs=[cols, rows(D_ATT, cur), cols, rows(D_CONV, lag), last(tt), last(tt),
                   last(HALO)],
        out_shape=[jax.ShapeDtypeStruct((b, n_t, D_ATT, tt), BF16),
                   jax.ShapeDtypeStruct((b, t, D_ATT), BF16),
                   jax.ShapeDtypeStruct((b, n_t, D_ATT, tt), BF16),
                   jax.ShapeDtypeStruct((b, t, D_CONV), BF16),
                   jax.ShapeDtypeStruct((b, tt, D_ATT), F32),
                   jax.ShapeDtypeStruct((b, tt, D_ATT), F32),
                   jax.ShapeDtypeStruct((b, HALO, D_CONV), F32)],
        scratch_shapes=[pltpu.VMEM((D_CONV // LANES, HALO + tt, LANES), F32),
                        pltpu.VMEM((tt, D_CONV), F32), pltpu.VMEM((tt, D_CONV), F32)],
        compiler_params=_params(2),
        name="mixer_in_prompt",
    )(x, g, w_in, conv_w, conv_b, cln_g, cln_b)


def _mixer_in_sample_kernel(x_ref, g_ref, w_ref, q_ref, k_ref, v_ref, u_ref):
    q, k, v, u, _ = _mixer_proj(x_ref, g_ref, w_ref)
    q_ref[...] = q.astype(BF16)
    k_ref[...] = k
    v_ref[...] = v
    u_ref[...] = u


def _mixer_in_sample(x, g, w_in):
    n, d = x.shape
    d_in = w_in.shape[1]
    full = lambda c: pl.BlockSpec((n, c), lambda i: (0, 0))
    return pl.pallas_call(
        _mixer_in_sample_kernel,
        grid=(1,),
        in_specs=[full(d), _const_spec((1, d)), _const_spec((d, d_in))],
        out_specs=[full(D_ATT), full(D_ATT), full(D_ATT), full(D_CONV)],
        out_shape=[jax.ShapeDtypeStruct((n, D_ATT), BF16),
                   jax.ShapeDtypeStruct((n, D_ATT), F32),
                   jax.ShapeDtypeStruct((n, D_ATT), F32),
                   jax.ShapeDtypeStruct((n, D_CONV), F32)],
        compiler_params=_params(1),
        name="mixer_in_sample",
    )(x, g, w_in)


def _stack_head_pair(q2):
    lane = lax.broadcasted_iota(jnp.int32, q2.shape, 1)
    zero = jnp.zeros_like(q2)
    return jnp.concatenate([jnp.where(lane < ATT_HEAD_DIM, q2, zero),
                            jnp.where(lane >= ATT_HEAD_DIM, q2, zero)], axis=0)


def _merge_head_pair(o):
    m = o.shape[0] // 2
    lane = lax.broadcasted_iota(jnp.int32, (m, LANES), 1)
    return jnp.where(lane < ATT_HEAD_DIM, o[:m], o[m:])


def _band_attn_kernel(qt_ref, kp_ref, kc_ref, vtp_ref, vtc_ref, bias_ref, o_ref, s_buf, p_buf):
    first = pl.program_id(1) == 0
    row = lax.broadcasted_iota(jnp.int32, (LANES, GROUP), 0)
    items = [(grp, p) for grp in range(ATT_STEP // GROUP) for p in range(ATT_HEADS // 2)]
    tiles_per_block = ATT_STEP // MXU_COLS
    n_lt = 2 * GROUP // LANES

    back = ATT_PAST // MXU_COLS

    def tile_slice(tile):
        t = tile % tiles_per_block
        return slice(t * MXU_COLS, (t + 1) * MXU_COLS)

    def in_tiles(ref, rows, col0, width):
        return ref[col0 // ATT_PAST, rows, col0 % ATT_PAST:col0 % ATT_PAST + width]

    @pl.when(first & (pl.program_id(0) == 0))
    def _():
        for lt in range(n_lt):
            z0 = KEY_WINDOW if lt % (GROUP // LANES) == 0 else 0
            p_buf[:, lt, z0:z0 + LANES, :] = jnp.zeros((2, LANES, LANES), BF16)

    def scores(n):
        grp, p = items[n]
        rows = slice(p * LANES, (p + 1) * LANES)
        qt2 = in_tiles(qt_ref, rows, grp * GROUP, GROUP)
        zero = jnp.zeros_like(qt2)
        qm = jnp.concatenate([jnp.where(row < ATT_HEAD_DIM, qt2, zero),
                              jnp.where(row >= ATT_HEAD_DIM, qt2, zero)], axis=1)
        for j in range(N_KEY_TILES):
            tile = tiles_per_block - back + grp + j
            k_ref = kp_ref if tile < tiles_per_block else kc_ref
            s = _dot(k_ref[tile_slice(tile), rows], qm)
            for lt in range(n_lt):
                w0 = LANES * (lt % (GROUP // LANES))
                lo, hi = max(j * MXU_COLS, w0), min((j + 1) * MXU_COLS, w0 + KEY_WINDOW)
                s_buf[n % 2, lt, lo:hi, :] = s[lo - j * MXU_COLS:hi - j * MXU_COLS,
                                               lt * LANES:(lt + 1) * LANES]

    def finish(n):
        grp, p = items[n]
        slot = n % 2
        rows = slice(p * LANES, (p + 1) * LANES)
        variant = jnp.where(first, 1 + grp, 0) if grp < back else 0
        for lt in range(n_lt):
            hh, cp = divmod(lt, GROUP // LANES)
            w0 = LANES * cp
            t = s_buf[slot, lt, w0:w0 + KEY_WINDOW, :] + bias_ref[variant, 2 * p + hh, cp]
            m = jnp.max(t, axis=0, keepdims=True)
            p_buf[slot, lt, w0:w0 + KEY_WINDOW, :] = jnp.exp2(t - m).astype(BF16)
        ones = jnp.ones((2 * SUBLANES, MXU_COLS), BF16)
        o = None
        for j in range(N_KEY_TILES):
            tile = tiles_per_block - back + grp + j
            vt_ref = vtp_ref if tile < tiles_per_block else vtc_ref
            pt = jnp.concatenate([p_buf[slot, lt, j * MXU_COLS:(j + 1) * MXU_COLS, :]
                                  for lt in range(n_lt)], axis=1)
            vt2 = in_tiles(vt_ref, rows, (tile % tiles_per_block) * MXU_COLS, MXU_COLS)
            d = _dot(jnp.concatenate([vt2, ones], axis=0), pt)
            o = d if o is None else o + d
        o = o[0:LANES] * (1.0 / o[LANES:LANES + 1])
        ot = jnp.where(row < ATT_HEAD_DIM, o[:, 0:GROUP], o[:, GROUP:2 * GROUP])
        o_ref[grp * GROUP:(grp + 1) * GROUP, rows] = ot.T.astype(BF16)

    scores(0)
    for n in range(len(items)):
        if n + 1 < len(items):
            scores(n + 1)
        finish(n)


def _toeplitz_bias(rel_bias, n_rows, n_cols, shift):
    n_heads = rel_bias.shape[0]
    period = n_cols + n_rows
    e0 = shift - REL_CLIP
    pad = period + abs(e0)
    table = jnp.pad(rel_bias[:, ::-1].astype(F32), ((0, 0), (pad, pad)), mode="edge")
    base = pad - e0
    vec = jnp.concatenate([table[:, base:base + n_cols + 1],
                           table[:, base - (n_rows - 1):base]], axis=1)
    flat = jnp.tile(vec, (1, n_rows))[:, :n_rows * (period - 1)]
    return flat.reshape(n_heads, n_rows, period - 1)[:, :, :n_cols]


def _band_bias_prompt(rel_bias):
    base = _toeplitz_bias(rel_bias, CHUNK, KEY_WINDOW + CHUNK, ATT_PAST + CHUNK) * LOG2E
    base = jnp.swapaxes(base, 1, 2)
    window = jnp.concatenate([base[:, CHUNK:CHUNK + KEY_WINDOW], base[:, 0:KEY_WINDOW]], axis=2)

    n_cp = GROUP // LANES
    cp = np.arange(n_cp)[None, :, None, None]
    kk = LANES * cp + np.arange(KEY_WINDOW)[None, None, :, None]
    kj = kk - CHUNK * (2 * cp + np.arange(LANES)[None, None, None, :] // CHUNK)
    first_valid = np.array([0, ATT_PAST, ATT_PAST - GROUP])[:, None, None, None]
    valid = (kj >= 0) & (kj < BAND) & (kk >= first_valid)
    return jnp.where(valid[:, None], window[None, :, None], NEG_INF)


def _band_attn_prompt(qt, k, vt, bias):
    b, t, _ = k.shape
    prev = lambda g: jnp.maximum(g - 1, 0)
    cols = lambda f: pl.BlockSpec((None, ATT_STEP // ATT_PAST, D_ATT, ATT_PAST),
                                  lambda i, g: (i, f(g), 0, 0))
    rows = lambda f: pl.BlockSpec((None, ATT_STEP, D_ATT), lambda i, g: (i, f(g), 0))
    same = lambda g: g
    return pl.pallas_call(
        _band_attn_kernel,
        grid=(b, t // ATT_STEP),
        in_specs=[cols(same), rows(prev), rows(same), cols(prev), cols(same),
                  _const_spec(bias.shape)],
        out_specs=rows(same),
        out_shape=jax.ShapeDtypeStruct((b, t, D_ATT), BF16),
        scratch_shapes=[pltpu.VMEM((2, 2 * GROUP // LANES, N_KEY_TILES * MXU_COLS, LANES), F32),
                        pltpu.VMEM((2, 2 * GROUP // LANES, N_KEY_TILES * MXU_COLS, LANES), BF16)],
        compiler_params=_params(2),
        name="band_attn_prompt",
    )(qt, k, k, vt, vt, bias)


def _shift_in(old_ref, new_ref, t_s):
    past = old_ref.shape[1]
    lane = lax.broadcasted_iota(jnp.int32, (old_ref.shape[0], LANES), 1)
    rolled = [pltpu.roll(old_ref[:, j * LANES:(j + 1) * LANES], LANES - t_s, axis=1)
              for j in range(past // LANES)]
    tail = rolled[1:] + [new_ref[...]]
    return jnp.concatenate([jnp.where(lane < LANES - t_s, a, b) for a, b in zip(rolled, tail)],
                           axis=1)


def _band_attn_sample_kernel(q_ref, kn_ref, vn_ref, ck_ref, cv_ref, bias_ref, bias0_ref,
                             o_ref, nk_ref, nv_ref, *, t_s):
    for s in range(q_ref.shape[0]):
        _band_attn_stream(q_ref.at[s], kn_ref.at[s], vn_ref.at[s], ck_ref.at[s], cv_ref.at[s],
                          bias_ref, bias0_ref, o_ref.at[s], nk_ref.at[s], nv_ref.at[s], t_s)


def _band_attn_stream(q_ref, kn_ref, vn_ref, ck_ref, cv_ref, bias_ref, bias0_ref,
                      o_ref, nk_ref, nv_ref, t_s):
    nk = _shift_in(ck_ref, kn_ref, t_s)
    nv = _shift_in(cv_ref, vn_ref, t_s)
    nk_ref[...] = nk
    nv_ref[...] = nv
    keys = (ck_ref[:, 0:LANES].astype(BF16), nk.astype(BF16))
    vals = (cv_ref[:, 0:LANES].astype(BF16), nv.astype(BF16))
    for p in range(ATT_HEADS // 2):
        sl = slice(p * LANES, (p + 1) * LANES)
        qm = _stack_head_pair(q_ref[:, sl])
        s = [_dot(qm, kt[sl, :]) + jnp.concatenate([b[2 * p], b[2 * p + 1]], axis=0)
             for kt, b in zip(keys, (bias0_ref, bias_ref))]
        m = jnp.maximum(*[jnp.max(x, axis=-1, keepdims=True) for x in s])
        e = [jnp.exp2(x - m) for x in s]
        inv_l = 1.0 / sum(jnp.sum(x, axis=-1, keepdims=True) for x in e)
        o = sum(_dot_nt(x.astype(BF16), vt[sl, :]) for x, vt in zip(e, vals)) * inv_l
        o_ref[:, sl] = _merge_head_pair(o).astype(BF16)


def _band_attn_sample(q, kn_t, vn_t, cache_kt, cache_vt, rel_bias):
    b, t_s, _ = q.shape
    past = cache_kt.shape[2]
    bias = _toeplitz_bias(rel_bias, t_s, past, ATT_PAST - t_s) * LOG2E
    bias0 = jnp.where(jnp.arange(LANES) < t_s,
                      _toeplitz_bias(rel_bias, t_s, LANES, ATT_PAST) * LOG2E, NEG_INF)
    per = SAMPLE_STREAMS_PER_STEP
    rows = pl.BlockSpec((per, t_s, D_ATT), lambda i: (i, 0, 0))
    new = pl.BlockSpec((per, D_ATT, LANES), lambda i: (i, 0, 0))
    buf = pl.BlockSpec((per, D_ATT, past), lambda i: (i, 0, 0))
    return pl.pallas_call(
        functools.partial(_band_attn_sample_kernel, t_s=t_s),
        grid=(b // per,),
        in_specs=[rows, new, new, buf, buf, _const_spec(bias.shape), _const_spec(bias0.shape)],
        out_specs=[rows, buf, buf],
        out_shape=[jax.ShapeDtypeStruct((b, t_s, D_ATT), BF16),
                   jax.ShapeDtypeStruct((b, D_ATT, past), F32),
                   jax.ShapeDtypeStruct((b, D_ATT, past), F32)],
        compiler_params=_params(1),
        name="band_attn_sample",
    )(q, kn_t, vn_t, cache_kt, cache_vt, bias, bias0)


def _out_proj_residual(att, c, x, wo_ref, gp_ref):
    y = _dot(att, wo_ref[0:D_ATT, :]) + _dot(c, wo_ref[D_ATT:D_ATT + D_CONV, :])
    return x + _rms(y, gp_ref[...])


def _mem_attn_heads(streams, a_buf):
    blocks = [(q, r0, mk, mv, hh) for q, r0, mk, mv in streams for hh in range(MEM_HEADS)]
    cols = lambda hh: slice(hh * MEM_HEAD_DIM, (hh + 1) * MEM_HEAD_DIM)
    scores = [_dot_nt(q[:, cols(hh)], mk(hh)) for q, _, mk, _, hh in blocks]
    probs = []
    for s in scores:
        e = jnp.exp2(s - jnp.max(s, axis=-1, keepdims=True))
        probs.append((e.astype(BF16), 1.0 / jnp.sum(e, axis=-1, keepdims=True)))
    for (q, r0, _, mv, hh), (e, inv_l) in zip(blocks, probs):
        a_buf[r0:r0 + q.shape[0], cols(hh)] = (_dot(e, mv(hh)) * inv_l).astype(BF16)


def _stored_mem_head(ref, b, hh):
    halves = MEM_HEAD_DIM // LANES
    parts = [ref[b, pl.ds(dh * MEM_HEADS + hh, N_MEM, stride=halves * MEM_HEADS), :]
             for dh in range(halves)]
    return jnp.concatenate(parts, axis=1).astype(BF16)


def _mem_query(x, gpre_ref, wq_ref):
    h = _rms(x, gpre_ref[...]).astype(BF16)
    return (_dot(h, wq_ref[...]) * (MEM_HEAD_DIM ** -0.5 * LOG2E)).astype(BF16)


def _sample_tail_kernel(att_ref, u_ref, halo_ref, x_ref, cw_ref, cb_ref, lg_ref, lb_ref,
                        wout_ref, gmix_ref, mk_ref, mv_ref, gpre_ref, wq_ref, wo_ref, gpost_ref,
                        o_ref, u_buf, conv_buf, c_buf, a_buf, *, t_s):
    n_streams = u_ref.shape[0]
    for b in range(n_streams):
        for c in range(D_CONV // LANES):
            sl = slice(c * LANES, (c + 1) * LANES)
            u_buf[c, b, 0:HALO, :] = halo_ref[b, :, sl]
            u_buf[c, b, HALO:HALO + t_s, :] = u_ref[b, :, sl]
        window = lambda c, start, b=b: u_buf[c, b, pl.ds(start, SUBLANES, stride=1), :]
        rows = pl.ds(b * t_s, t_s)
        _conv_module(window, cw_ref, cb_ref, lg_ref, lb_ref, conv_buf.at[rows], c_buf.at[rows], t_s)
    x = _out_proj_residual(att_ref[...], c_buf[...], x_ref[...], wout_ref, gmix_ref)
    q = _mem_query(x, gpre_ref, wq_ref)
    _mem_attn_heads([(q[b * t_s:(b + 1) * t_s, :], b * t_s,
                      functools.partial(_stored_mem_head, mk_ref, b),
                      functools.partial(_stored_mem_head, mv_ref, b))
                     for b in range(n_streams)], a_buf)
    o_ref[...] = x + _rms(_dot(a_buf[...], wo_ref[...]), gpost_ref[...])


def _sample_tail(att, u, halo, x, conv_w, conv_b, cln_g, cln_b, w_out, g_mix_post,
                 mk, mv, g_pre, w_mq, w_mo, g_post):
    b_s, t_s, _ = u.shape
    n, d = x.shape
    steps = SAMPLE_TAIL_STEPS
    sb, sn = b_s // steps, n // steps
    streams = lambda a: pl.BlockSpec((sb,) + a.shape[1:], lambda i: (i,) + (0,) * (a.ndim - 1))
    rows = lambda a: pl.BlockSpec((sn, a.shape[1]), lambda i: (i, 0))
    const = lambda a: _const_spec(a.shape)
    operands = (att, u, halo, x, conv_w, conv_b, cln_g, cln_b, w_out, g_mix_post,
                mk, mv, g_pre, w_mq, w_mo, g_post)
    specs = (rows, streams, streams, rows, const, const, const, const, const, const,
             streams, streams, const, const, const, const)
    return pl.pallas_call(
        functools.partial(_sample_tail_kernel, t_s=t_s),
        grid=(steps,),
        in_specs=[spec(a) for spec, a in zip(specs, operands)],
        out_specs=pl.BlockSpec((sn, d), lambda i: (i, 0)),
        out_shape=jax.ShapeDtypeStruct((n, d), F32),
        scratch_shapes=[pltpu.VMEM((D_CONV // LANES, sb, HALO + t_s, LANES), F32),
                        pltpu.VMEM((sn, D_CONV), F32), pltpu.VMEM((sn, D_CONV), BF16),
                        pltpu.VMEM((sn, d), BF16)],
        compiler_params=_params(1),
        name="sample_tail",
    )(*operands)


def _mix_mem_attn_kernel(att_ref, c_ref, x_ref, wout_ref, gmix_ref, mk_ref, mv_ref, gpre_ref,
                         wq_ref, wo_ref, gpost_ref, o_ref, a_buf):
    tt = x_ref.shape[0]
    halves = [slice(r0, r0 + tt // 2) for r0 in (0, tt // 2)]
    xs = [_out_proj_residual(att_ref[r, :], c_ref[r, :], x_ref[r, :], wout_ref, gmix_ref)
          for r in halves]
    qs = [_mem_query(x, gpre_ref, wq_ref) for x in xs]
    cols = lambda hh: slice(hh * MEM_HEAD_DIM, (hh + 1) * MEM_HEAD_DIM)
    mk = lambda hh: mk_ref[:, cols(hh)]
    mv = lambda hh: mv_ref[:, cols(hh)]
    _mem_attn_heads([(q, r.start, mk, mv) for q, r in zip(qs, halves)], a_buf)
    for r, x in zip(halves, xs):
        o_ref[r, :] = x + _rms(_dot(a_buf[r, :], wo_ref[...]), gpost_ref[...])


def _mix_mem_attn(att, c, x, w_out, g_mix_post, mk, mv, g_pre, w_mq, w_mo, g_post, *, tt):
    b, t, d = x.shape
    rows = lambda n: pl.BlockSpec((None, tt, n), lambda i, j: (i, j, 0))
    mem = pl.BlockSpec((None, N_MEM, d), lambda i, j: (i, 0, 0))
    return pl.pallas_call(
        _mix_mem_attn_kernel,
        grid=(b, t // tt),
        in_specs=[rows(D_ATT), rows(D_CONV), rows(d), _const_spec(w_out.shape), _const_spec((1, d)),
                  mem, mem, _const_spec((1, d)), _const_spec((d, d)), _const_spec((d, d)),
                  _const_spec((1, d))],
        out_specs=rows(d),
        out_shape=jax.ShapeDtypeStruct((b, t, d), F32),
        scratch_shapes=[pltpu.VMEM((tt, d), BF16)],
        compiler_params=_params(2),
        name="mix_mem_attn",
    )(att, c, x, w_out, g_mix_post, mk, mv, g_pre, w_mq, w_mo, g_post)


def _ffn_kernel(x_ref, gpre_ref, wg_ref, wu_ref, wd_ref, gpost_ref, o_ref, acc_ref):
    tt = x_ref.shape[0]
    halves = [slice(r0, r0 + tt // 2) for r0 in (0, tt // 2)]
    xs = [x_ref[rows, :] for rows in halves]
    hs = [_rms(x, gpre_ref[...]).astype(BF16) for x in xs]
    n_chunks = D_FF // FF_CHUNK
    cols = lambda c: slice(c * FF_CHUNK, (c + 1) * FF_CHUNK)
    gate_up = lambda c: [(_dot(h, wg_ref[:, cols(c)]), _dot(h, wu_ref[:, cols(c)])) for h in hs]
    nxt = gate_up(0)
    for c in range(n_chunks):
        cur = nxt
        if c + 1 < n_chunks:
            nxt = gate_up(c + 1)
        for rows, (gate, up) in zip(halves, cur):
            a = (gate * jax.nn.sigmoid(gate) * up).astype(BF16)
            part = _dot(a, wd_ref[cols(c), :])
            if c == 0:
                acc_ref[rows, :] = part
            else:
                acc_ref[rows, :] += part
    for rows, x in zip(halves, xs):
        o_ref[rows, :] = x + _rms(acc_ref[rows, :], gpost_ref[...])


def _ffn(x, g_pre, w_gate, w_up, w_down, g_post, *, tt):
    n, d = x.shape
    rows = pl.BlockSpec((tt, d), lambda i: (i, 0))
    return pl.pallas_call(
        _ffn_kernel,
        grid=(n // tt,),
        in_specs=[rows, _const_spec((1, d)), _const_spec((d, D_FF)), _const_spec((d, D_FF)),
                  _const_spec((D_FF, d)), _const_spec((1, d))],
        out_specs=rows,
        out_shape=jax.ShapeDtypeStruct((n, d), F32),
        scratch_shapes=[pltpu.VMEM((tt, d), F32)],
        compiler_params=_params(1),
        name="ffn",
    )(x, g_pre, w_gate, w_up, w_down, g_post)


def kernel(x_prompt, x_sample, cache_att_k, cache_att_v, cache_conv, cache_mem_k, cache_mem_v, mem_prompt, g_mix_pre, g_mix_post, w_in, rel_bias, conv_w, conv_b, cln_g, cln_b, w_out, g_mem_pre, g_mem_post, g_mem_kv, w_mq, w_mk, w_mv, w_mo, g_ffn_pre, g_ffn_post, w_gate, w_up, w_down):
    depth = w_in.shape[0]
    b, t_p, d = x_prompt.shape
    b_s, t_s, _ = x_sample.shape
    past = cache_att_k.shape[2]
    assert d == D_MODEL and past == ATT_PAST
    assert t_p % ATT_STEP == 0 and t_p % MIX_ROWS == 0 and (b * t_p) % FFN_ROWS == 0
    assert t_s <= CHUNK and t_s % 16 == 0 and CONV_WIDTH - 1 <= HALO

    row = lambda a: a[None, :]
    xp, xs = x_prompt, x_sample
    outs = [[] for _ in range(8)]
    for l in range(depth):
        bf = lambda w: w[l].astype(BF16)
        w_in_l, w_out_l = bf(w_in), bf(w_out)
        w_mq_l, w_mo_l = bf(w_mq), bf(w_mo)
        w_gate_l, w_up_l, w_down_l = bf(w_gate), bf(w_up), bf(w_down)
        conv = (conv_w[l], row(conv_b[l]), row(cln_g[l]), row(cln_b[l]))

        mk_p, mv_p, mk_pb, mv_pb = _mem_kv(mem_prompt, row(g_mem_kv[l]), w_mk[l], w_mv[l])
        qt, k, vt, c_act, k_last, v_last, u_last = _mixer_in_prompt(
            xp, row(g_mix_pre[l]), w_in_l, *conv)
        att = _band_attn_prompt(qt, k, vt, _band_bias_prompt(rel_bias[l]))
        mem_w = (row(g_mem_pre[l]), w_mq_l, w_mo_l, row(g_mem_post[l]))
        xp = _mix_mem_attn(att, c_act, xp, w_out_l, row(g_mix_post[l]), mk_pb, mv_pb, *mem_w,
                           tt=MIX_ROWS)
        xp = _ffn(xp.reshape(b * t_p, d), row(g_ffn_pre[l]), w_gate_l, w_up_l, w_down_l,
                  row(g_ffn_post[l]), tt=FFN_ROWS).reshape(b, t_p, d)
        outs[0].append(k_last.reshape(b, ATT_PAST, ATT_HEADS, ATT_HEAD_DIM))
        outs[1].append(v_last.reshape(b, ATT_PAST, ATT_HEADS, ATT_HEAD_DIM))
        outs[2].append(u_last[:, HALO - (CONV_WIDTH - 1):])
        outs[3].append(mk_p.reshape(b, N_MEM, MEM_HEADS, MEM_HEAD_DIM))
        outs[4].append(mv_p.reshape(b, N_MEM, MEM_HEADS, MEM_HEAD_DIM))

        q, k_new, v_new, u = _mixer_in_sample(xs.reshape(b_s * t_s, d), row(g_mix_pre[l]), w_in_l)
        u = u.reshape(b_s, t_s, D_CONV)
        to_t = lambda a: jnp.swapaxes(a.reshape(b_s, -1, D_ATT), 1, 2)
        new_cols = lambda a: jnp.pad(to_t(a), ((0, 0), (0, 0), (LANES - t_s, 0)))
        att, nk, nv = _band_attn_sample(
            q.reshape(b_s, t_s, D_ATT), new_cols(k_new), new_cols(v_new),
            to_t(cache_att_k[l]), to_t(cache_att_v[l]), rel_bias[l])
        nk, nv = jnp.swapaxes(nk, 1, 2), jnp.swapaxes(nv, 1, 2)
        halo = jnp.pad(cache_conv[l], ((0, 0), (HALO - (CONV_WIDTH - 1), 0), (0, 0)))
        halves = MEM_HEAD_DIM // LANES
        stored = lambda c: jnp.swapaxes(c.reshape(b_s, N_MEM, MEM_HEADS, halves, LANES), 2, 3).reshape(
            b_s, N_MEM * halves * MEM_HEADS, LANES)
        xs = _sample_tail(att.reshape(b_s * t_s, D_ATT), u, halo, xs.reshape(b_s * t_s, d), *conv,
                          w_out_l, row(g_mix_post[l]), stored(cache_mem_k[l]),
                          stored(cache_mem_v[l]), *mem_w)
        xs = _ffn(xs, row(g_ffn_pre[l]), w_gate_l, w_up_l, w_down_l,
                  row(g_ffn_post[l]), tt=b_s * t_s).reshape(b_s, t_s, d)
        outs[5].append(nk.reshape(b_s, past, ATT_HEADS, ATT_HEAD_DIM))
        outs[6].append(nv.reshape(b_s, past, ATT_HEADS, ATT_HEAD_DIM))
        outs[7].append(jnp.concatenate([cache_conv[l], u], axis=1)[:, -(CONV_WIDTH - 1):])

    return (xp, xs) + tuple(jnp.stack(o, 0) for o in outs)
```

```python
import functools

import jax
import jax.numpy as jnp
import numpy as np
from jax import lax
from jax.experimental import pallas as pl
from jax.experimental.pallas import tpu as pltpu

D_MODEL = 1024
CHUNK = 64
ATT_PAST = 512
BAND = ATT_PAST + CHUNK
D_ATT = 512
ATT_HEADS = 8
ATT_HEAD_DIM = 64
D_CONV = 512
CONV_WIDTH = 31
REL_CLIP = 128
N_MEM = 256
MEM_HEADS = 4
MEM_HEAD_DIM = 256
D_FF = 2816
EPS = 1e-6
NEG_INF = -1e30

LANES = 128
SUBLANES = 8
MXU_COLS = 256
GROUP = 4 * CHUNK
N_KEY_TILES = (ATT_PAST + GROUP) // MXU_COLS
ATT_STEP = 4 * ATT_PAST
LOG2E = 1.4426950408889634
Q_SCALE = ATT_HEAD_DIM ** -0.5 * LOG2E
KEY_WINDOW = BAND + CHUNK
HALO = 32
FF_CHUNK = MXU_COLS
FFN_ROWS = 1024
MIX_ROWS = 1024
SAMPLE_STREAMS_PER_STEP = 2
SAMPLE_TAIL_STEPS = 2
VMEM_LIMIT = 56 * 1024 * 1024

F32 = jnp.float32
BF16 = jnp.bfloat16


def _params(n_grid):
    return pltpu.CompilerParams(dimension_semantics=("arbitrary",) * n_grid,
                                vmem_limit_bytes=VMEM_LIMIT)


def _const_spec(shape):
    zeros = (0,) * len(shape)
    return pl.BlockSpec(shape, lambda *_: zeros, pipeline_mode=pl.Buffered(1))


def _dot(a, b):
    return jnp.dot(a, b, preferred_element_type=F32)


def _dot_nt(a, b):
    return lax.dot_general(a, b, (((1,), (1,)), ((), ())), preferred_element_type=F32)


def _rms(x, g):
    y = x * lax.rsqrt(jnp.mean(x * x, axis=-1, keepdims=True) + EPS)
    return y * g


def _mem_kv_kernel(m_ref, g_ref, wk_ref, wv_ref, k_ref, v_ref, kb_ref, vb_ref):
    m = _rms(m_ref[...], g_ref[...]).astype(BF16)
    k = _dot(m, wk_ref[...].astype(BF16))
    v = _dot(m, wv_ref[...].astype(BF16))
    k_ref[...] = k
    v_ref[...] = v
    kb_ref[...] = k.astype(BF16)
    vb_ref[...] = v.astype(BF16)


def _mem_kv(mem, g, wk, wv):
    b, n, d = mem.shape
    rows = pl.BlockSpec((b * n, d), lambda i: (0, 0))
    outs = pl.pallas_call(
        _mem_kv_kernel,
        grid=(1,),
        in_specs=[rows, _const_spec((1, d)), _const_spec((d, d)), _const_spec((d, d))],
        out_specs=[rows, rows, rows, rows],
        out_shape=[jax.ShapeDtypeStruct((b * n, d), F32)] * 2
        + [jax.ShapeDtypeStruct((b * n, d), BF16)] * 2,
        compiler_params=_params(1),
        name="mem_kv",
    )(mem.reshape(b * n, d), g, wk, wv)
    return [o.reshape(b, n, d) for o in outs]


def _mixer_proj(x_ref, g_ref, w_ref):
    h = _rms(x_ref[...], g_ref[...]).astype(BF16)
    edges = (0, D_ATT, 2 * D_ATT, 3 * D_ATT, 3 * D_ATT + D_CONV, 3 * D_ATT + 2 * D_CONV)
    raw = [_dot(h, w_ref[:, a:b]) for a, b in zip(edges[:-1], edges[1:])]
    q, k, v, u_val, u_gate = raw
    return q * Q_SCALE, k, v, u_val * jax.nn.sigmoid(u_gate), raw


def _zero_after(x):
    bits = lax.bitcast_convert_type(x, jnp.uint32)
    zero = lax.shift_right_logical(lax.shift_right_logical(bits, jnp.uint32(16)), jnp.uint32(16))
    return lax.bitcast_convert_type(zero, F32)


def _conv_module(window, cw_ref, cb_ref, lg_ref, lb_ref, conv_buf, out_ref, tt, gate=None):
    lead = HALO - (CONV_WIDTH - 1)
    rs = min(tt, CHUNK)
    blocks = rs // SUBLANES
    strips = D_CONV // LANES
    for r0 in range(0, tt, rs):
        for c in range(strips):
            sl = slice(c * LANES, (c + 1) * LANES)
            taps = [jnp.broadcast_to(cw_ref[w:w + 1, sl], (SUBLANES, LANES))
                    for w in range(CONV_WIDTH)]
            bias = jnp.broadcast_to(cb_ref[:, sl], (SUBLANES, LANES))
            if gate is not None:
                bias = bias + gate((r0 // rs) * strips + c)
            accs = [bias] * blocks
            for d in range(SUBLANES * (blocks - 1) + CONV_WIDTH):
                win = window(c, r0 + lead + d)
                for b in range(blocks):
                    w = d - SUBLANES * b
                    if 0 <= w < CONV_WIDTH:
                        accs[b] = accs[b] + win * taps[w]
            for b in range(blocks):
                conv_buf[r0 + b * SUBLANES:r0 + (b + 1) * SUBLANES, sl] = accs[b]

        acc = conv_buf[r0:r0 + rs, :]
        mu = jnp.mean(acc, axis=-1, keepdims=True)
        xc = acc - mu
        var = jnp.mean(xc * xc, axis=-1, keepdims=True)
        y = xc * lax.rsqrt(var + EPS) * lg_ref[...] + lb_ref[...]
        out_ref[r0:r0 + rs, :] = (y * jax.nn.sigmoid(y)).astype(BF16)


def _mixer_in_prompt_kernel(x_ref, g_ref, w_ref, cw_ref, cb_ref, lg_ref, lb_ref,
                            qt_ref, k_ref, vt_ref, c_ref, kl_ref, vl_ref, ul_ref,
                            u_buf, u_next, conv_buf, *, tt):
    strips = D_CONV // LANES

    @pl.when(pl.program_id(1) == 0)
    def _():
        u_buf[...] = jnp.zeros(u_buf.shape, F32)

    q, k, v, u, raw = _mixer_proj(x_ref, g_ref, w_ref)
    qt_ref[...] = q.T.astype(BF16)
    k_ref[...] = k.astype(BF16)
    vt_ref[...] = v.T.astype(BF16)
    kl_ref[...] = k
    vl_ref[...] = v
    ul_ref[...] = u[tt - HALO:tt, :]
    u_next[...] = u

    n_units = (tt // CHUNK) * strips
    col_tiles = D_ATT // MXU_COLS
    n_pass = len(raw) * col_tiles

    def gate(n):
        pos = (n + 1) * n_pass / (n_units + 1)
        dot_i, nt = divmod(int(pos), col_tiles)
        r = int((pos - int(pos)) * (tt // SUBLANES)) * SUBLANES
        return _zero_after(raw[dot_i][r:r + SUBLANES, nt * MXU_COLS:nt * MXU_COLS + LANES])

    window = lambda c, start: u_buf[c, pl.ds(start, SUBLANES, stride=1), :]
    _conv_module(window, cw_ref, cb_ref, lg_ref, lb_ref, conv_buf, c_ref, tt, gate=gate)

    for c in range(strips):
        u_buf[c, 0:HALO, :] = u_buf[c, tt:tt + HALO, :]
        u_buf[c, HALO:HALO + tt, :] = u_next[:, c * LANES:(c + 1) * LANES]


def _mixer_in_prompt(x, g, w_in, conv_w, conv_b, cln_g, cln_b):
    b, t, d = x.shape
    tt = ATT_PAST
    n_t = t // tt
    d_in = w_in.shape[1]
    cur = lambda j: jnp.minimum(j, n_t - 1)
    lag = lambda j: jnp.maximum(j - 1, 0)
    rows = lambda c, f: pl.BlockSpec((None, tt, c), lambda i, j: (i, f(j), 0))
    cols = pl.BlockSpec((None, None, D_ATT, tt), lambda i, j: (i, cur(j), 0, 0))
    last = lambda r: pl.BlockSpec((None, r, D_ATT), lambda i, j: (i, 0, 0))
    return pl.pallas_call(
        functools.partial(_mixer_in_prompt_kernel, tt=tt),
        grid=(b, n_t + 1),
        in_specs=[rows(d, cur), _const_spec((1, d)), _const_spec((d, d_in)),
                  _const_spec(conv_w.shape), _const_spec((1, D_CONV)), _const_spec((1, D_CONV)),
                  _const_spec((1, D_CONV))],
        out_specs=[cols, rows(D_ATT, cur), cols, rows(D_CONV, lag), last(tt), last(tt),
                   last(HALO)],
        out_shape=[jax.ShapeDtypeStruct((b, n_t, D_ATT, tt), BF16),
                   jax.ShapeDtypeStruct((b, t, D_ATT), BF16),
                   jax.ShapeDtypeStruct((b, n_t, D_ATT, tt), BF16),
                   jax.ShapeDtypeStruct((b, t, D_CONV), BF16),
                   jax.ShapeDtypeStruct((b, tt, D_ATT), F32),
                   jax.ShapeDtypeStruct((b, tt, D_ATT), F32),
                   jax.ShapeDtypeStruct((b, HALO, D_CONV), F32)],
        scratch_shapes=[pltpu.VMEM((D_CONV // LANES, HALO + tt, LANES), F32),
                        pltpu.VMEM((tt, D_CONV), F32), pltpu.VMEM((tt, D_CONV), F32)],
        compiler_params=_params(2),
        name="mixer_in_prompt",
    )(x, g, w_in, conv_w, conv_b, cln_g, cln_b)


def _mixer_in_sample_kernel(x_ref, g_ref, w_ref, q_ref, k_ref, v_ref, u_ref):
    q, k, v, u, _ = _mixer_proj(x_ref, g_ref, w_ref)
    q_ref[...] = q.astype(BF16)
    k_ref[...] = k
    v_ref[...] = v
    u_ref[...] = u


def _mixer_in_sample(x, g, w_in):
    n, d = x.shape
    d_in = w_in.shape[1]
    full = lambda c: pl.BlockSpec((n, c), lambda i: (0, 0))
    return pl.pallas_call(
        _mixer_in_sample_kernel,
        grid=(1,),
        in_specs=[full(d), _const_spec((1, d)), _const_spec((d, d_in))],
        out_specs=[full(D_ATT), full(D_ATT), full(D_ATT), full(D_CONV)],
        out_shape=[jax.ShapeDtypeStruct((n, D_ATT), BF16),
                   jax.ShapeDtypeStruct((n, D_ATT), F32),
                   jax.ShapeDtypeStruct((n, D_ATT), F32),
                   jax.ShapeDtypeStruct((n, D_CONV), F32)],
        compiler_params=_params(1),
        name="mixer_in_sample",
    )(x, g, w_in)


def _stack_head_pair(q2):
    lane = lax.broadcasted_iota(jnp.int32, q2.shape, 1)
    zero = jnp.zeros_like(q2)
    return jnp.concatenate([jnp.where(lane < ATT_HEAD_DIM, q2, zero),
                            jnp.where(lane >= ATT_HEAD_DIM, q2, zero)], axis=0)


def _merge_head_pair(o):
    m = o.shape[0] // 2
    lane = lax.broadcasted_iota(jnp.int32, (m, LANES), 1)
    return jnp.where(lane < ATT_HEAD_DIM, o[:m], o[m:])


def _band_attn_kernel(qt_ref, kp_ref, kc_ref, vtp_ref, vtc_ref, bias_ref, o_ref, s_buf, p_buf):
    first = pl.program_id(1) == 0
    row = lax.broadcasted_iota(jnp.int32, (LANES, GROUP), 0)
    items = [(grp, p) for grp in range(ATT_STEP // GROUP) for p in range(ATT_HEADS // 2)]
    tiles_per_block = ATT_STEP // MXU_COLS
    n_lt = 2 * GROUP // LANES

    back = ATT_PAST // MXU_COLS

    def tile_slice(tile):
        t = tile % tiles_per_block
        return slice(t * MXU_COLS, (t + 1) * MXU_COLS)

    def in_tiles(ref, rows, col0, width):
        return ref[col0 // ATT_PAST, rows, col0 % ATT_PAST:col0 % ATT_PAST + width]

    @pl.when(first & (pl.program_id(0) == 0))
    def _():
        for lt in range(n_lt):
            z0 = KEY_WINDOW if lt % (GROUP // LANES) == 0 else 0
            p_buf[:, lt, z0:z0 + LANES, :] = jnp.zeros((2, LANES, LANES), BF16)

    def scores(n):
        grp, p = items[n]
        rows = slice(p * LANES, (p + 1) * LANES)
        qt2 = in_tiles(qt_ref, rows, grp * GROUP, GROUP)
        zero = jnp.zeros_like(qt2)
        qm = jnp.concatenate([jnp.where(row < ATT_HEAD_DIM, qt2, zero),
                              jnp.where(row >= ATT_HEAD_DIM, qt2, zero)], axis=1)
        for j in range(N_KEY_TILES):
            tile = tiles_per_block - back + grp + j
            k_ref = kp_ref if tile < tiles_per_block else kc_ref
            s = _dot(k_ref[tile_slice(tile), rows], qm)
            if tile < tiles_per_block:
                s = jnp.where(first, NEG_INF, s)
            for lt in range(n_lt):
                w0 = LANES * (lt % (GROUP // LANES))
                lo, hi = max(j * MXU_COLS, w0), min((j + 1) * MXU_COLS, w0 + KEY_WINDOW)
                s_buf[n % 2, lt, lo:hi, :] = s[lo - j * MXU_COLS:hi - j * MXU_COLS,
                                               lt * LANES:(lt + 1) * LANES]

    def finish(n):
        grp, p = items[n]
        slot = n % 2
        rows = slice(p * LANES, (p + 1) * LANES)
        for lt in range(n_lt):
            hh, cp = divmod(lt, GROUP // LANES)
            w0 = LANES * cp
            t = s_buf[slot, lt, w0:w0 + KEY_WINDOW, :] + bias_ref[2 * p + hh, cp]
            m = jnp.max(t, axis=0, keepdims=True)
            p_buf[slot, lt, w0:w0 + KEY_WINDOW, :] = jnp.exp2(t - m).astype(BF16)
        ones = jnp.ones((2 * SUBLANES, MXU_COLS), BF16)
        o = None
        for j in range(N_KEY_TILES):
            tile = tiles_per_block - back + grp + j
            vt_ref = vtp_ref if tile < tiles_per_block else vtc_ref
            pt = jnp.concatenate([p_buf[slot, lt, j * MXU_COLS:(j + 1) * MXU_COLS, :]
                                  for lt in range(n_lt)], axis=1)
            vt2 = in_tiles(vt_ref, rows, (tile % tiles_per_block) * MXU_COLS, MXU_COLS)
            d = _dot(jnp.concatenate([vt2, ones], axis=0), pt)
            o = d if o is None else o + d
        o = o[0:LANES] * (1.0 / o[LANES:LANES + 1])
        ot = jnp.where(row < ATT_HEAD_DIM, o[:, 0:GROUP], o[:, GROUP:2 * GROUP])
        o_ref[grp * GROUP:(grp + 1) * GROUP, rows] = ot.T.astype(BF16)

    scores(0)
    for n in range(len(items)):
        if n + 1 < len(items):
            scores(n + 1)
        finish(n)


def _toeplitz_bias(rel_bias, n_rows, n_cols, shift):
    n_heads = rel_bias.shape[0]
    period = n_cols + n_rows
    e0 = shift - REL_CLIP
    pad = period + abs(e0)
    table = jnp.pad(rel_bias[:, ::-1].astype(F32), ((0, 0), (pad, pad)), mode="edge")
    base = pad - e0
    vec = jnp.concatenate([table[:, base:base + n_cols + 1],
                           table[:, base - (n_rows - 1):base]], axis=1)
    flat = jnp.tile(vec, (1, n_rows))[:, :n_rows * (period - 1)]
    return flat.reshape(n_heads, n_rows, period - 1)[:, :, :n_cols]


def _band_bias_prompt(rel_bias):
    base = _toeplitz_bias(rel_bias, CHUNK, KEY_WINDOW + CHUNK, ATT_PAST + CHUNK) * LOG2E
    base = jnp.swapaxes(base, 1, 2)
    window = jnp.concatenate([base[:, CHUNK:CHUNK + KEY_WINDOW], base[:, 0:KEY_WINDOW]], axis=2)

    n_cp = GROUP // LANES
    cp = np.arange(n_cp)[:, None, None]
    kk = LANES * cp + np.arange(KEY_WINDOW)[None, :, None]
    kj = kk - CHUNK * (2 * cp + np.arange(LANES)[None, None, :] // CHUNK)
    valid = (kj >= 0) & (kj < BAND)
    return jnp.where(valid[None], window[:, None], NEG_INF)


def _band_attn_prompt(qt, k, vt, bias):
    b, t, _ = k.shape
    prev = lambda g: jnp.maximum(g - 1, 0)
    cols = lambda f: pl.BlockSpec((None, ATT_STEP // ATT_PAST, D_ATT, ATT_PAST),
                                  lambda i, g: (i, f(g), 0, 0))
    rows = lambda f: pl.BlockSpec((None, ATT_STEP, D_ATT), lambda i, g: (i, f(g), 0))
    same = lambda g: g
    return pl.pallas_call(
        _band_attn_kernel,
        grid=(b, t // ATT_STEP),
        in_specs=[cols(same), rows(prev), rows(same), cols(prev), cols(same),
                  _const_spec(bias.shape)],
        out_specs=rows(same),
        out_shape=jax.ShapeDtypeStruct((b, t, D_ATT), BF16),
        scratch_shapes=[pltpu.VMEM((2, 2 * GROUP // LANES, N_KEY_TILES * MXU_COLS, LANES), F32),
                        pltpu.VMEM((2, 2 * GROUP // LANES, N_KEY_TILES * MXU_COLS, LANES), BF16)],
        compiler_params=_params(2),
        name="band_attn_prompt",
    )(qt, k, k, vt, vt, bias)


def _shift_in(old_ref, new_ref, t_s):
    past = old_ref.shape[1]
    lane = lax.broadcasted_iota(jnp.int32, (old_ref.shape[0], LANES), 1)
    rolled = [pltpu.roll(old_ref[:, j * LANES:(j + 1) * LANES], LANES - t_s, axis=1)
              for j in range(past // LANES)]
    tail = rolled[1:] + [new_ref[...]]
    return jnp.concatenate([jnp.where(lane < LANES - t_s, a, b) for a, b in zip(rolled, tail)],
                           axis=1)


def _band_attn_sample_kernel(q_ref, kn_ref, vn_ref, ck_ref, cv_ref, bias_ref, bias0_ref,
                             o_ref, nk_ref, nv_ref, *, t_s):
    for s in range(q_ref.shape[0]):
        _band_attn_stream(q_ref.at[s], kn_ref.at[s], vn_ref.at[s], ck_ref.at[s], cv_ref.at[s],
                          bias_ref, bias0_ref, o_ref.at[s], nk_ref.at[s], nv_ref.at[s], t_s)


def _band_attn_stream(q_ref, kn_ref, vn_ref, ck_ref, cv_ref, bias_ref, bias0_ref,
                      o_ref, nk_ref, nv_ref, t_s):
    nk = _shift_in(ck_ref, kn_ref, t_s)
    nv = _shift_in(cv_ref, vn_ref, t_s)
    nk_ref[...] = nk
    nv_ref[...] = nv
    keys = (ck_ref[:, 0:LANES].astype(BF16), nk.astype(BF16))
    vals = (cv_ref[:, 0:LANES].astype(BF16), nv.astype(BF16))
    for p in range(ATT_HEADS // 2):
        sl = slice(p * LANES, (p + 1) * LANES)
        qm = _stack_head_pair(q_ref[:, sl])
        s = [_dot(qm, kt[sl, :]) + jnp.concatenate([b[2 * p], b[2 * p + 1]], axis=0)
             for kt, b in zip(keys, (bias0_ref, bias_ref))]
        m = jnp.maximum(*[jnp.max(x, axis=-1, keepdims=True) for x in s])
        e = [jnp.exp2(x - m) for x in s]
        inv_l = 1.0 / sum(jnp.sum(x, axis=-1, keepdims=True) for x in e)
        o = sum(_dot_nt(x.astype(BF16), vt[sl, :]) for x, vt in zip(e, vals)) * inv_l
        o_ref[:, sl] = _merge_head_pair(o).astype(BF16)


def _band_attn_sample(q, kn_t, vn_t, cache_kt, cache_vt, rel_bias):
    b, t_s, _ = q.shape
    past = cache_kt.shape[2]
    bias = _toeplitz_bias(rel_bias, t_s, past, ATT_PAST - t_s) * LOG2E
    bias0 = jnp.where(jnp.arange(LANES) < t_s,
                      _toeplitz_bias(rel_bias, t_s, LANES, ATT_PAST) * LOG2E, NEG_INF)
    per = SAMPLE_STREAMS_PER_STEP
    rows = pl.BlockSpec((per, t_s, D_ATT), lambda i: (i, 0, 0))
    new = pl.BlockSpec((per, D_ATT, LANES), lambda i: (i, 0, 0))
    buf = pl.BlockSpec((per, D_ATT, past), lambda i: (i, 0, 0))
    return pl.pallas_call(
        functools.partial(_band_attn_sample_kernel, t_s=t_s),
        grid=(b // per,),
        in_specs=[rows, new, new, buf, buf, _const_spec(bias.shape), _const_spec(bias0.shape)],
        out_specs=[rows, buf, buf],
        out_shape=[jax.ShapeDtypeStruct((b, t_s, D_ATT), BF16),
                   jax.ShapeDtypeStruct((b, D_ATT, past), F32),
                   jax.ShapeDtypeStruct((b, D_ATT, past), F32)],
        compiler_params=_params(1),
        name="band_attn_sample",
    )(q, kn_t, vn_t, cache_kt, cache_vt, bias, bias0)


def _out_proj_residual(att, c, x, wo_ref, gp_ref):
    y = _dot(att, wo_ref[0:D_ATT, :]) + _dot(c, wo_ref[D_ATT:D_ATT + D_CONV, :])
    return x + _rms(y, gp_ref[...])


def _mem_attn_heads(streams, a_buf):
    blocks = [(q, r0, mk, mv, hh) for q, r0, mk, mv in streams for hh in range(MEM_HEADS)]
    cols = lambda hh: slice(hh * MEM_HEAD_DIM, (hh + 1) * MEM_HEAD_DIM)
    scores = [_dot_nt(q[:, cols(hh)], mk(hh)) for q, _, mk, _, hh in blocks]
    probs = []
    for s in scores:
        e = jnp.exp2(s - jnp.max(s, axis=-1, keepdims=True))
        probs.append((e.astype(BF16), 1.0 / jnp.sum(e, axis=-1, keepdims=True)))
    for (q, r0, _, mv, hh), (e, inv_l) in zip(blocks, probs):
        a_buf[r0:r0 + q.shape[0], cols(hh)] = (_dot(e, mv(hh)) * inv_l).astype(BF16)


def _stored_mem_head(ref, b, hh):
    halves = MEM_HEAD_DIM // LANES
    parts = [ref[b, pl.ds(dh * MEM_HEADS + hh, N_MEM, stride=halves * MEM_HEADS), :]
             for dh in range(halves)]
    return jnp.concatenate(parts, axis=1).astype(BF16)


def _mem_query(x, gpre_ref, wq_ref):
    h = _rms(x, gpre_ref[...]).astype(BF16)
    return (_dot(h, wq_ref[...]) * (MEM_HEAD_DIM ** -0.5 * LOG2E)).astype(BF16)


def _sample_tail_kernel(att_ref, u_ref, halo_ref, x_ref, cw_ref, cb_ref, lg_ref, lb_ref,
                        wout_ref, gmix_ref, mk_ref, mv_ref, gpre_ref, wq_ref, wo_ref, gpost_ref,
                        o_ref, u_buf, conv_buf, c_buf, a_buf, *, t_s):
    n_streams = u_ref.shape[0]
    for b in range(n_streams):
        for c in range(D_CONV // LANES):
            sl = slice(c * LANES, (c + 1) * LANES)
            u_buf[c, b, 0:HALO, :] = halo_ref[b, :, sl]
            u_buf[c, b, HALO:HALO + t_s, :] = u_ref[b, :, sl]
        window = lambda c, start, b=b: u_buf[c, b, pl.ds(start, SUBLANES, stride=1), :]
        rows = pl.ds(b * t_s, t_s)
        _conv_module(window, cw_ref, cb_ref, lg_ref, lb_ref, conv_buf.at[rows], c_buf.at[rows], t_s)
    x = _out_proj_residual(att_ref[...], c_buf[...], x_ref[...], wout_ref, gmix_ref)
    q = _mem_query(x, gpre_ref, wq_ref)
    _mem_attn_heads([(q[b * t_s:(b + 1) * t_s, :], b * t_s,
                      functools.partial(_stored_mem_head, mk_ref, b),
                      functools.partial(_stored_mem_head, mv_ref, b))
                     for b in range(n_streams)], a_buf)
    o_ref[...] = x + _rms(_dot(a_buf[...], wo_ref[...]), gpost_ref[...])


def _sample_tail(att, u, halo, x, conv_w, conv_b, cln_g, cln_b, w_out, g_mix_post,
                 mk, mv, g_pre, w_mq, w_mo, g_post):
    b_s, t_s, _ = u.shape
    n, d = x.shape
    steps = SAMPLE_TAIL_STEPS
    sb, sn = b_s // steps, n // steps
    streams = lambda a: pl.BlockSpec((sb,) + a.shape[1:], lambda i: (i,) + (0,) * (a.ndim - 1))
    rows = lambda a: pl.BlockSpec((sn, a.shape[1]), lambda i: (i, 0))
    const = lambda a: _const_spec(a.shape)
    operands = (att, u, halo, x, conv_w, conv_b, cln_g, cln_b, w_out, g_mix_post,
                mk, mv, g_pre, w_mq, w_mo, g_post)
    specs = (rows, streams, streams, rows, const, const, const, const, const, const,
             streams, streams, const, const, const, const)
    return pl.pallas_call(
        functools.partial(_sample_tail_kernel, t_s=t_s),
        grid=(steps,),
        in_specs=[spec(a) for spec, a in zip(specs, operands)],
        out_specs=pl.BlockSpec((sn, d), lambda i: (i, 0)),
        out_shape=jax.ShapeDtypeStruct((n, d), F32),
        scratch_shapes=[pltpu.VMEM((D_CONV // LANES, sb, HALO + t_s, LANES), F32),
                        pltpu.VMEM((sn, D_CONV), F32), pltpu.VMEM((sn, D_CONV), BF16),
                        pltpu.VMEM((sn, d), BF16)],
        compiler_params=_params(1),
        name="sample_tail",
    )(*operands)


def _mix_mem_attn_kernel(att_ref, c_ref, x_ref, wout_ref, gmix_ref, mk_ref, mv_ref, gpre_ref,
                         wq_ref, wo_ref, gpost_ref, o_ref, a_buf):
    tt = x_ref.shape[0]
    halves = [slice(r0, r0 + tt // 2) for r0 in (0, tt // 2)]
    xs = [_out_proj_residual(att_ref[r, :], c_ref[r, :], x_ref[r, :], wout_ref, gmix_ref)
          for r in halves]
    qs = [_mem_query(x, gpre_ref, wq_ref) for x in xs]
    cols = lambda hh: slice(hh * MEM_HEAD_DIM, (hh + 1) * MEM_HEAD_DIM)
    mk = lambda hh: mk_ref[:, cols(hh)]
    mv = lambda hh: mv_ref[:, cols(hh)]
    _mem_attn_heads([(q, r.start, mk, mv) for q, r in zip(qs, halves)], a_buf)
    for r, x in zip(halves, xs):
        o_ref[r, :] = x + _rms(_dot(a_buf[r, :], wo_ref[...]), gpost_ref[...])


def _mix_mem_attn(att, c, x, w_out, g_mix_post, mk, mv, g_pre, w_mq, w_mo, g_post, *, tt):
    b, t, d = x.shape
    rows = lambda n: pl.BlockSpec((None, tt, n), lambda i, j: (i, j, 0))
    mem = pl.BlockSpec((None, N_MEM, d), lambda i, j: (i, 0, 0))
    return pl.pallas_call(
        _mix_mem_attn_kernel,
        grid=(b, t // tt),
        in_specs=[rows(D_ATT), rows(D_CONV), rows(d), _const_spec(w_out.shape), _const_spec((1, d)),
                  mem, mem, _const_spec((1, d)), _const_spec((d, d)), _const_spec((d, d)),
                  _const_spec((1, d))],
        out_specs=rows(d),
        out_shape=jax.ShapeDtypeStruct((b, t, d), F32),
        scratch_shapes=[pltpu.VMEM((tt, d), BF16)],
        compiler_params=_params(2),
        name="mix_mem_attn",
    )(att, c, x, w_out, g_mix_post, mk, mv, g_pre, w_mq, w_mo, g_post)


def _ffn_kernel(x_ref, gpre_ref, wg_ref, wu_ref, wd_ref, gpost_ref, o_ref, acc_ref):
    tt = x_ref.shape[0]
    halves = [slice(r0, r0 + tt // 2) for r0 in (0, tt // 2)]
    _ffn_rows([x_ref[rows, :] for rows in halves], halves, gpre_ref, wg_ref, wu_ref, wd_ref,
              gpost_ref, o_ref, acc_ref)


def _ffn_rows(xs, halves, gpre_ref, wg_ref, wu_ref, wd_ref, gpost_ref, o_ref, acc_ref):
    hs = [_rms(x, gpre_ref[...]).astype(BF16) for x in xs]
    n_chunks = D_FF // FF_CHUNK
    cols = lambda c: slice(c * FF_CHUNK, (c + 1) * FF_CHUNK)
    gate_up = lambda c: [(_dot(h, wg_ref[:, cols(c)]), _dot(h, wu_ref[:, cols(c)])) for h in hs]
    nxt = gate_up(0)
    for c in range(n_chunks):
        cur = nxt
        if c + 1 < n_chunks:
            nxt = gate_up(c + 1)
        for rows, (gate, up) in zip(halves, cur):
            a = (gate * jax.nn.sigmoid(gate) * up).astype(BF16)
            part = _dot(a, wd_ref[cols(c), :])
            if c == 0:
                acc_ref[rows, :] = part
            else:
                acc_ref[rows, :] += part
    for rows, x in zip(halves, xs):
        o_ref[rows, :] = x + _rms(acc_ref[rows, :], gpost_ref[...])


def _ffn(x, g_pre, w_gate, w_up, w_down, g_post, *, tt):
    n, d = x.shape
    rows = pl.BlockSpec((tt, d), lambda i: (i, 0))
    return pl.pallas_call(
        _ffn_kernel,
        grid=(n // tt,),
        in_specs=[rows, _const_spec((1, d)), _const_spec((d, D_FF)), _const_spec((d, D_FF)),
                  _const_spec((D_FF, d)), _const_spec((1, d))],
        out_specs=rows,
        out_shape=jax.ShapeDtypeStruct((n, d), F32),
        scratch_shapes=[pltpu.VMEM((tt, d), F32)],
        compiler_params=_params(1),
        name="ffn",
    )(x, g_pre, w_gate, w_up, w_down, g_post)


def _mix_mem_ffn_kernel(att_ref, c_ref, x_ref, wout_ref, gmix_ref, mk_ref, mv_ref, gpre_ref,
                        wq_ref, wo_ref, gpost_ref, fpre_ref, wg_ref, wu_ref, wd_ref, fpost_ref,
                        o_ref, a_buf, acc_ref):
    tt = x_ref.shape[0]
    halves = [slice(r0, r0 + tt // 2) for r0 in (0, tt // 2)]
    xs = [_out_proj_residual(att_ref[r, :], c_ref[r, :], x_ref[r, :], wout_ref, gmix_ref)
          for r in halves]
    qs = [_mem_query(x, gpre_ref, wq_ref) for x in xs]
    cols = lambda hh: slice(hh * MEM_HEAD_DIM, (hh + 1) * MEM_HEAD_DIM)
    mk = lambda hh: mk_ref[:, cols(hh)]
    mv = lambda hh: mv_ref[:, cols(hh)]
    _mem_attn_heads([(q, r.start, mk, mv) for q, r in zip(qs, halves)], a_buf)
    xs = [x + _rms(_dot(a_buf[r, :], wo_ref[...]), gpost_ref[...]) for r, x in zip(halves, xs)]
    _ffn_rows(xs, halves, fpre_ref, wg_ref, wu_ref, wd_ref, fpost_ref, o_ref, acc_ref)


def _mix_mem_ffn(att, c, x, w_out, g_mix_post, mk, mv, g_pre, w_mq, w_mo, g_post,
                 f_pre, w_gate, w_up, w_down, f_post, *, tt):
    b, t, d = x.shape
    rows = lambda n: pl.BlockSpec((None, tt, n), lambda i, j: (i, j, 0))
    mem = pl.BlockSpec((None, N_MEM, d), lambda i, j: (i, 0, 0))
    vec = _const_spec((1, d))
    return pl.pallas_call(
        _mix_mem_ffn_kernel,
        grid=(b, t // tt),
        in_specs=[rows(D_ATT), rows(D_CONV), rows(d), _const_spec(w_out.shape), vec,
                  mem, mem, vec, _const_spec((d, d)), _const_spec((d, d)), vec,
                  vec, _const_spec((d, D_FF)), _const_spec((d, D_FF)), _const_spec((D_FF, d)), vec],
        out_specs=rows(d),
        out_shape=jax.ShapeDtypeStruct((b, t, d), F32),
        scratch_shapes=[pltpu.VMEM((tt, d), BF16), pltpu.VMEM((tt, d), F32)],
        compiler_params=_params(2),
        name="mix_mem_ffn",
    )(att, c, x, w_out, g_mix_post, mk, mv, g_pre, w_mq, w_mo, g_post,
      f_pre, w_gate, w_up, w_down, f_post)


def kernel(x_prompt, x_sample, cache_att_k, cache_att_v, cache_conv, cache_mem_k, cache_mem_v, mem_prompt, g_mix_pre, g_mix_post, w_in, rel_bias, conv_w, conv_b, cln_g, cln_b, w_out, g_mem_pre, g_mem_post, g_mem_kv, w_mq, w_mk, w_mv, w_mo, g_ffn_pre, g_ffn_post, w_gate, w_up, w_down):
    depth = w_in.shape[0]
    b, t_p, d = x_prompt.shape
    b_s, t_s, _ = x_sample.shape
    past = cache_att_k.shape[2]
    assert d == D_MODEL and past == ATT_PAST
    assert t_p % ATT_STEP == 0 and t_p % MIX_ROWS == 0 and (b * t_p) % FFN_ROWS == 0
    assert t_s <= CHUNK and t_s % 16 == 0 and CONV_WIDTH - 1 <= HALO

    row = lambda a: a[None, :]
    xp, xs = x_prompt, x_sample
    outs = [[] for _ in range(8)]
    for l in range(depth):
        bf = lambda w: w[l].astype(BF16)
        w_in_l, w_out_l = bf(w_in), bf(w_out)
        w_mq_l, w_mo_l = bf(w_mq), bf(w_mo)
        w_gate_l, w_up_l, w_down_l = bf(w_gate), bf(w_up), bf(w_down)
        conv = (conv_w[l], row(conv_b[l]), row(cln_g[l]), row(cln_b[l]))

        mk_p, mv_p, mk_pb, mv_pb = _mem_kv(mem_prompt, row(g_mem_kv[l]), w_mk[l], w_mv[l])
        qt, k, vt, c_act, k_last, v_last, u_last = _mixer_in_prompt(
            xp, row(g_mix_pre[l]), w_in_l, *conv)
        att = _band_attn_prompt(qt, k, vt, _band_bias_prompt(rel_bias[l]))
        mem_w = (row(g_mem_pre[l]), w_mq_l, w_mo_l, row(g_mem_post[l]))
        ffn_w = (row(g_ffn_pre[l]), w_gate_l, w_up_l, w_down_l, row(g_ffn_post[l]))
        xp = _mix_mem_ffn(att, c_act, xp, w_out_l, row(g_mix_post[l]), mk_pb, mv_pb, *mem_w,
                          *ffn_w, tt=ATT_PAST)
        outs[0].append(k_last.reshape(b, ATT_PAST, ATT_HEADS, ATT_HEAD_DIM))
        outs[1].append(v_last.reshape(b, ATT_PAST, ATT_HEADS, ATT_HEAD_DIM))
        outs[2].append(u_last[:, HALO - (CONV_WIDTH - 1):])
        outs[3].append(mk_p.reshape(b, N_MEM, MEM_HEADS, MEM_HEAD_DIM))
        outs[4].append(mv_p.reshape(b, N_MEM, MEM_HEADS, MEM_HEAD_DIM))

        q, k_new, v_new, u = _mixer_in_sample(xs.reshape(b_s * t_s, d), row(g_mix_pre[l]), w_in_l)
        u = u.reshape(b_s, t_s, D_CONV)
        to_t = lambda a: jnp.swapaxes(a.reshape(b_s, -1, D_ATT), 1, 2)
        new_cols = lambda a: jnp.pad(to_t(a), ((0, 0), (0, 0), (LANES - t_s, 0)))
        att, nk, nv = _band_attn_sample(
            q.reshape(b_s, t_s, D_ATT), new_cols(k_new), new_cols(v_new),
            to_t(cache_att_k[l]), to_t(cache_att_v[l]), rel_bias[l])
        nk, nv = jnp.swapaxes(nk, 1, 2), jnp.swapaxes(nv, 1, 2)
        halo = jnp.pad(cache_conv[l], ((0, 0), (HALO - (CONV_WIDTH - 1), 0), (0, 0)))
        halves = MEM_HEAD_DIM // LANES
        stored = lambda c: jnp.swapaxes(c.reshape(b_s, N_MEM, MEM_HEADS, halves, LANES), 2, 3).reshape(
            b_s, N_MEM * halves * MEM_HEADS, LANES)
        xs = _sample_tail(att.reshape(b_s * t_s, D_ATT), u, halo, xs.reshape(b_s * t_s, d), *conv,
                          w_out_l, row(g_mix_post[l]), stored(cache_mem_k[l]),
                          stored(cache_mem_v[l]), *mem_w)
        xs = _ffn(xs, row(g_ffn_pre[l]), w_gate_l, w_up_l, w_down_l,
                  row(g_ffn_post[l]), tt=b_s * t_s).reshape(b_s, t_s, d)
        outs[5].append(nk.reshape(b_s, past, ATT_HEADS, ATT_HEAD_DIM))
        outs[6].append(nv.reshape(b_s, past, ATT_HEADS, ATT_HEAD_DIM))
        outs[7].append(jnp.concatenate([cache_conv[l], u], axis=1)[:, -(CONV_WIDTH - 1):])

    return (xp, xs) + tuple(jnp.stack(o, 0) for o in outs)
```
